```python
import jax, jax.numpy as jnp
from jax import lax
import numpy as np

D_MODEL = 2048
BATCH = 2
SEQ = 16384
DEPTH = 1

GLA_HEADS = 4
GLA_DK = D_MODEL // 2 // GLA_HEADS
GLA_DV = D_MODEL // GLA_HEADS
GLA_GATE_RANK = 16
GLA_TAU = 16.0
GLA_CHUNK = 64

SWA_HEADS = 32
SWA_KV_HEADS = 4
SWA_HEAD_DIM = 64
SWA_GROUP = SWA_HEADS // SWA_KV_HEADS
WINDOW = 128

D_FF = 4 * D_MODEL

DEEPNORM_ALPHA = (2.0 * DEPTH) ** 0.25
DEEPNORM_BETA = (8.0 * DEPTH) ** -0.25
LN_EPS = 1e-5
RMS_EPS = 1e-6

SPLITS = (
    GLA_HEADS * GLA_DK,
    GLA_HEADS * GLA_DK,
    GLA_HEADS * GLA_DV,
    GLA_HEADS * GLA_DV,
    GLA_GATE_RANK,
    SWA_HEADS * SWA_HEAD_DIM,
    SWA_KV_HEADS * SWA_HEAD_DIM,
    SWA_KV_HEADS * SWA_HEAD_DIM,
    D_MODEL,
    D_MODEL,
)
V_SEGMENTS = (2, 7)

kernel_name = "hybrid_gla_swa_deepnorm_layer"


def layer_norm(x, g, b):
    xf = x.astype(jnp.float32)
    mu = jnp.mean(xf, axis=-1, keepdims=True)
    var = jnp.mean(jnp.square(xf - mu), axis=-1, keepdims=True)
    y = (xf - mu) * lax.rsqrt(var + LN_EPS) * g.astype(jnp.float32) + b.astype(jnp.float32)
    return y.astype(x.dtype)


def alibi_slopes(n_heads):
    h = np.arange(1, n_heads + 1, dtype=np.float32)
    return jnp.asarray(2.0 ** (-8.0 * h / n_heads), dtype=jnp.float32)


def gla_chunked(q, k, v, log_a):
    B, S, H, DK = q.shape
    DV = v.shape[-1]
    C = GLA_CHUNK
    nc = S // C
    q = q.reshape(B, nc, C, H, DK)
    k = k.reshape(B, nc, C, H, DK)
    v = v.reshape(B, nc, C, H, DV)
    b = jnp.cumsum(log_a.reshape(B, nc, C, H, DK), axis=2)
    b_last = b[:, :, -1]
    q_in = q * jnp.exp(b)
    k_in = k * jnp.exp(-b)
    k_dec = k * jnp.exp(b_last[:, :, None] - b)
    decay = jnp.exp(b_last)
    causal = jnp.tril(jnp.ones((C, C), dtype=bool))
    scores = jnp.einsum('bnqhd,bnshd->bnhqs', q_in, k_in)
    scores = jnp.where(causal, scores, 0.0)
    o_intra = jnp.einsum('bnhqs,bnshv->bnqhv', scores, v)

    def step(state, xs):
        qc, kc, vc, dc = xs
        o = jnp.einsum('bqhd,bhdv->bqhv', qc, state)
        state = dc[..., None] * state + jnp.einsum('bshd,bshv->bhdv', kc, vc)
        return state, o

    xs = (jnp.moveaxis(q_in, 1, 0), jnp.moveaxis(k_dec, 1, 0),
          jnp.moveaxis(v, 1, 0), jnp.moveaxis(decay, 1, 0))
    state0 = jnp.zeros((B, H, DK, DV), dtype=q.dtype)
    _, o_inter = lax.scan(step, state0, xs)
    o = o_intra + jnp.moveaxis(o_inter, 0, 1)
    return o.reshape(B, S, H, DV)


def swa_banded(q, k, v, sinks):
    B, S, H, hd = q.shape
    W = WINDOW
    nb = S // W
    qb = q.reshape(B, nb, W, SWA_KV_HEADS, SWA_GROUP, hd)
    pad = ((0, 0), (W, 0), (0, 0), (0, 0))
    kp = jnp.pad(k, pad).reshape(B, nb + 1, W, SWA_KV_HEADS, hd)
    vp = jnp.pad(v, pad).reshape(B, nb + 1, W, SWA_KV_HEADS, hd)
    kb = jnp.concatenate([kp[:, :-1], kp[:, 1:]], axis=2)
    vb = jnp.concatenate([vp[:, :-1], vp[:, 1:]], axis=2)
    logits = jnp.einsum('bnqkgd,bnskd->bnkgqs', qb, kb).astype(jnp.float32) * (hd ** -0.5)
    qi = jnp.arange(W)[:, None]
    sj = jnp.arange(2 * W)[None, :]
    dist = qi - sj + W
    blk = jnp.arange(nb)[:, None, None]
    valid = (dist >= 0) & (dist < W) & (blk * W + sj - W >= 0)
    slopes = alibi_slopes(SWA_HEADS).reshape(SWA_KV_HEADS, SWA_GROUP)
    logits = logits - slopes[:, :, None, None] * dist.astype(jnp.float32)
    logits = jnp.where(valid[None, :, None, None], logits, jnp.finfo(jnp.float32).min)
    sink = sinks.astype(jnp.float32).reshape(SWA_KV_HEADS, SWA_GROUP)[None, None, :, :, None, None]
    m = jnp.maximum(jnp.max(logits, axis=-1, keepdims=True), sink)
    p = jnp.exp(logits - m)
    probs = p / (jnp.sum(p, axis=-1, keepdims=True) + jnp.exp(sink - m))
    o = jnp.einsum('bnkgqs,bnskd->bnqkgd', probs.astype(v.dtype), vb)
    return o.reshape(B, S, H * hd)


def setup_inputs(seed: int = 0) -> dict:
    key = jax.random.key(seed)
    ks = jax.random.split(key, 16)
    f32 = jnp.float32
    n_in = int(sum(SPLITS))
    col_scale = np.concatenate([
        np.full((s,), DEEPNORM_BETA if i in V_SEGMENTS else 1.0, dtype=np.float32)
        for i, s in enumerate(SPLITS)])
    x = jax.random.normal(ks[0], (BATCH, SEQ, D_MODEL), f32)
    w_in = jax.random.normal(ks[1], (D_MODEL, n_in), f32) * (D_MODEL ** -0.5) * jnp.asarray(col_scale)
    w_alpha_up = jax.random.normal(ks[2], (GLA_GATE_RANK, GLA_HEADS * GLA_DK), f32) * (GLA_GATE_RANK ** -0.5)
    b_alpha = 0.1 * jax.random.normal(ks[3], (GLA_HEADS * GLA_DK,), f32)
    gla_norm_w = 1.0 + 0.02 * jax.random.normal(ks[4], (GLA_DV,), f32)
    attn_sinks = 0.5 * jax.random.normal(ks[5], (SWA_HEADS,), f32)
    w_branch_gla = jax.random.normal(ks[6], (GLA_HEADS * GLA_DV, D_MODEL), f32) * ((GLA_HEADS * GLA_DV) ** -0.5) * DEEPNORM_BETA
    w_branch_swa = jax.random.normal(ks[7], (SWA_HEADS * SWA_HEAD_DIM, D_MODEL), f32) * ((SWA_HEADS * SWA_HEAD_DIM) ** -0.5) * DEEPNORM_BETA
    w_out = jax.random.normal(ks[8], (D_MODEL, D_MODEL), f32) * (D_MODEL ** -0.5) * DEEPNORM_BETA
    ln1_g = 1.0 + 0.02 * jax.random.normal(ks[9], (D_MODEL,), f32)
    ln1_b = 0.02 * jax.random.normal(ks[10], (D_MODEL,), f32)
    w_ff_up = jax.random.normal(ks[11], (D_MODEL, D_FF), f32) * (D_MODEL ** -0.5)
    w_ff_down = jax.random.normal(ks[12], (D_FF, D_MODEL), f32) * (D_FF ** -0.5) * DEEPNORM_BETA
    ln2_g = 1.0 + 0.02 * jax.random.normal(ks[13], (D_MODEL,), f32)
    ln2_b = 0.02 * jax.random.normal(ks[14], (D_MODEL,), f32)
    return {"x": x, "w_in": w_in, "w_alpha_up": w_alpha_up, "b_alpha": b_alpha,
            "gla_norm_w": gla_norm_w, "attn_sinks": attn_sinks,
            "w_branch_gla": w_branch_gla, "w_branch_swa": w_branch_swa, "w_out": w_out,
            "ln1_g": ln1_g, "ln1_b": ln1_b, "w_ff_up": w_ff_up, "w_ff_down": w_ff_down,
            "ln2_g": ln2_g, "ln2_b": ln2_b}


def reference(x, w_in, w_alpha_up, b_alpha, gla_norm_w, attn_sinks, w_branch_gla,
              w_branch_swa, w_out, ln1_g, ln1_b, w_ff_up, w_ff_down, ln2_g, ln2_b):
    B, S, _ = x.shape
    f32 = jnp.float32
    offsets = [int(o) for o in np.cumsum(SPLITS)[:-1]]
    for _layer in range(DEPTH):
        proj = x @ w_in
        (g_q, g_k, g_v, g_out, g_lr, s_q, s_k, s_v,
         gate_gla, gate_swa) = jnp.split(proj, offsets, axis=-1)

        log_a = jax.nn.log_sigmoid((g_lr @ w_alpha_up + b_alpha).astype(f32)) / GLA_TAU
        q_a = g_q.astype(f32).reshape(B, S, GLA_HEADS, GLA_DK) * (GLA_DK ** -0.5)
        k_a = g_k.astype(f32).reshape(B, S, GLA_HEADS, GLA_DK)
        v_a = g_v.astype(f32).reshape(B, S, GLA_HEADS, GLA_DV)
        o_a = gla_chunked(q_a, k_a, v_a, log_a.reshape(B, S, GLA_HEADS, GLA_DK))
        o_a = o_a * lax.rsqrt(jnp.mean(jnp.square(o_a), axis=-1, keepdims=True) + RMS_EPS) * gla_norm_w.astype(f32)
        o_a = (o_a.reshape(B, S, GLA_HEADS * GLA_DV) * jax.nn.silu(g_out.astype(f32))).astype(x.dtype)
        y_a = o_a @ w_branch_gla

        q_b = s_q.reshape(B, S, SWA_HEADS, SWA_HEAD_DIM)
        k_b = s_k.reshape(B, S, SWA_KV_HEADS, SWA_HEAD_DIM)
        v_b = s_v.reshape(B, S, SWA_KV_HEADS, SWA_HEAD_DIM)
        o_b = swa_banded(q_b, k_b, v_b, attn_sinks)
        y_b = o_b @ w_branch_swa

        merged = jax.nn.sigmoid(gate_gla) * y_a + jax.nn.sigmoid(gate_swa) * y_b
        mix = merged @ w_out
        x = layer_norm(DEEPNORM_ALPHA * x + mix, ln1_g, ln1_b)

        h = jnp.square(jax.nn.relu(x @ w_ff_up))
        x = layer_norm(DEEPNORM_ALPHA * x + h @ w_ff_down, ln2_g, ln2_b)
    return x
```

```python
import functools

import jax
import jax.numpy as jnp
import numpy as np
from jax import lax
from jax.experimental import pallas as pl
from jax.experimental.pallas import tpu as pltpu

F32 = jnp.float32
BF16 = jnp.bfloat16

D_MODEL = 2048
GLA_HEADS = 4
GLA_DK = 256
GLA_DV = 512
GLA_RANK = 16
GLA_TAU = 16.0
GLA_CHUNK = 64
SWA_HEADS = 32
SWA_KV_HEADS = 4
SWA_HD = 64
SWA_GROUP = 8
WINDOW = 128
D_FF = 4 * D_MODEL
ALPHA = 2.0 ** 0.25
LN_EPS = 1e-5
RMS_EPS = 1e-6

Q_OFF, K_OFF, V_OFF, GO_OFF, SQ_OFF, GG_OFF, GS_OFF, SK_OFF, SV_OFF = (
    0, 1024, 2048, 4096, 6144, 8192, 10240, 12288, 12544)
N_PROJ = 12800
LANE = 128

VMEM_LIMIT = 56 * 1024 * 1024


def _layer_norm(z, g, b):
    mu = jnp.mean(z, axis=-1, keepdims=True)
    zc = z - mu
    var = jnp.mean(zc * zc, axis=-1, keepdims=True)
    return zc * lax.rsqrt(var + LN_EPS) * g + b


def _proj_kernel(x_ref, w_ref, wlr_ref, o_ref, lr_ref, xb_ref):
    @pl.when(pl.program_id(1) == 0)
    def _():
        xb = x_ref[...].astype(BF16)
        xb_ref[...] = xb
        lr_ref[...] = jnp.dot(xb, wlr_ref[...], preferred_element_type=F32)

    o_ref[...] = jnp.dot(xb_ref[...], w_ref[...],
                         preferred_element_type=F32).astype(BF16)


def _proj(x2, w_main, w_lr, tm=1024, tn=1280):
    m = x2.shape[0]
    return pl.pallas_call(
        _proj_kernel,
        grid=(m // tm, N_PROJ // tn),
        in_specs=[
            pl.BlockSpec((tm, D_MODEL), lambda i, j: (i, 0)),
            pl.BlockSpec((D_MODEL, tn), lambda i, j: (0, j)),
            pl.BlockSpec((D_MODEL, LANE), lambda i, j: (0, 0)),
        ],
        out_specs=[
            pl.BlockSpec((tm, tn), lambda i, j: (i, j)),
            pl.BlockSpec((tm, LANE), lambda i, j: (i, 0)),
        ],
        out_shape=[
            jax.ShapeDtypeStruct((m, N_PROJ), BF16),
            jax.ShapeDtypeStruct((m, LANE), F32),
        ],
        scratch_shapes=[pltpu.VMEM((tm, D_MODEL), BF16)],
        compiler_params=pltpu.CompilerParams(
            dimension_semantics=("arbitrary", "arbitrary"),
            vmem_limit_bytes=VMEM_LIMIT),
        name="proj",
    )(x2, w_main, w_lr)


def _gla_kernel(q_ref, k_ref, v_ref, go_ref, lr_ref, wup_ref, ba_ref, nw_ref,
                o_ref, st_ref, *, n_chunks):
    c = GLA_CHUNK

    @pl.when(pl.program_id(2) == 0)
    def _():
        st_ref[...] = jnp.zeros_like(st_ref)

    row = lax.broadcasted_iota(jnp.int32, (c, c), 0)
    col = lax.broadcasted_iota(jnp.int32, (c, c), 1)
    causal = row >= col
    tri = jnp.where(causal, 1.0, 0.0).astype(BF16)
    wup = wup_ref[...]
    ba = ba_ref[...]
    nw = nw_ref[...]

    def body(i, carry):
        r0 = pl.multiple_of(i * c, c)
        rows = pl.ds(r0, c)
        z = jnp.dot(lr_ref[rows, :].astype(BF16), wup,
                    preferred_element_type=F32) + ba
        la = jax.nn.log_sigmoid(z) * (1.0 / GLA_TAU)
        la_hi = la.astype(BF16)
        la_lo = (la - la_hi.astype(F32)).astype(BF16)
        b = (jnp.dot(tri, la_hi, preferred_element_type=F32)
             + jnp.dot(tri, la_lo, preferred_element_type=F32))
        b_last = b[c - 1:c, :]
        q = q_ref[rows, :].astype(F32) * (GLA_DK ** -0.5)
        k = k_ref[rows, :].astype(F32)
        v = v_ref[rows, :]
        q_in = (q * jnp.exp(b)).astype(BF16)
        k_in = (k * jnp.exp(-b)).astype(BF16)
        k_dec = (k * jnp.exp(b_last - b)).astype(BF16)
        decay = jnp.exp(b_last)
        scores = lax.dot_general(q_in, k_in, (((1,), (1,)), ((), ())),
                                 preferred_element_type=F32)
        scores = jnp.where(causal, scores, 0.0).astype(BF16)
        st = st_ref[...]
        o = (jnp.dot(scores, v, preferred_element_type=F32)
             + lax.dot_general(q_in, st.astype(BF16), (((1,), (1,)), ((), ())),
                               preferred_element_type=F32))
        upd = lax.dot_general(v, k_dec, (((0,), (0,)), ((), ())),
                              preferred_element_type=F32)
        st_ref[...] = st * decay + upd
        rms = lax.rsqrt(jnp.mean(o * o, axis=-1, keepdims=True) + RMS_EPS)
        g = go_ref[rows, :].astype(F32)
        o_ref[rows, :] = (o * rms * nw * (g * jax.nn.sigmoid(g))).astype(BF16)
        return carry

    lax.fori_loop(0, n_chunks, body, 0)


def _gla(proj, glr, wup_pad, b_alpha2, norm_w2, batch, seq, t_rows=512):
    m = proj.shape[0]
    nt = seq // t_rows
    row = lambda b, h, t: b * nt + t
    return pl.pallas_call(
        functools.partial(_gla_kernel, n_chunks=t_rows // GLA_CHUNK),
        grid=(batch, GLA_HEADS, nt),
        in_specs=[
            pl.BlockSpec((t_rows, GLA_DK), lambda b, h, t: (row(b, h, t), Q_OFF // GLA_DK + h)),
            pl.BlockSpec((t_rows, GLA_DK), lambda b, h, t: (row(b, h, t), K_OFF // GLA_DK + h)),
            pl.BlockSpec((t_rows, GLA_DV), lambda b, h, t: (row(b, h, t), V_OFF // GLA_DV + h)),
            pl.BlockSpec((t_rows, GLA_DV), lambda b, h, t: (row(b, h, t), GO_OFF // GLA_DV + h)),
            pl.BlockSpec((t_rows, LANE), lambda b, h, t: (row(b, h, t), 0)),
            pl.BlockSpec((LANE, GLA_DK), lambda b, h, t: (0, h)),
            pl.BlockSpec((1, GLA_DK), lambda b, h, t: (0, h)),
            pl.BlockSpec((1, GLA_DV), lambda b, h, t: (0, 0)),
        ],
        out_specs=pl.BlockSpec((t_rows, GLA_DV), lambda b, h, t: (row(b, h, t), h)),
        out_shape=jax.ShapeDtypeStruct((m, GLA_HEADS * GLA_DV), BF16),
        scratch_shapes=[pltpu.VMEM((GLA_DV, GLA_DK), F32)],
        compiler_params=pltpu.CompilerParams(
            dimension_semantics=("arbitrary", "arbitrary", "arbitrary"),
            vmem_limit_bytes=VMEM_LIMIT),
        name="gla",
    )(proj, proj, proj, proj, glr, wup_pad, b_alpha2, norm_w2)


def _alibi_slopes():
    h = np.arange(1, SWA_HEADS + 1, dtype=np.float32)
    return [float(s) for s in (2.0 ** (-8.0 * h / SWA_HEADS)).astype(np.float32)]


def _swa_kernel(sink_ref, q_ref, kp_ref, kc_ref, vp_ref, vc_ref, o_ref):
    w = WINDOW
    n = pl.program_id(1)
    slopes = _alibi_slopes()
    qi = lax.broadcasted_iota(jnp.int32, (w, 2 * w), 0)
    sj = lax.broadcasted_iota(jnp.int32, (w, 2 * w), 1)
    dist = qi - sj + w
    valid = (dist >= 0) & (dist < w) & ((sj >= w) | (n > 0))
    distf = dist.astype(F32)
    neg = jnp.finfo(F32).min
    lane = lax.broadcasted_iota(jnp.int32, (2 * w, LANE), 1)
    low = lane < SWA_HD
    lane_q = lax.broadcasted_iota(jnp.int32, (w, LANE), 1)
    low_q = lane_q < SWA_HD

    kfull = jnp.concatenate([kp_ref[...], kc_ref[...]], axis=0).astype(F32)
    vfull = jnp.concatenate([vp_ref[...], vc_ref[...]], axis=0).astype(F32)

    for kvp in range(SWA_KV_HEADS // 2):
        kcol = kfull[:, kvp * LANE:(kvp + 1) * LANE]
        vcol = vfull[:, kvp * LANE:(kvp + 1) * LANE]
        krol = pltpu.roll(kcol, SWA_HD, 1)
        vrol = pltpu.roll(vcol, SWA_HD, 1)
        for par in range(2):
            kv = 2 * kvp + par
            if par == 0:
                kdup = jnp.where(low, kcol, krol)
                vdup = jnp.where(low, vcol, vrol)
            else:
                kdup = jnp.where(low, krol, kcol)
                vdup = jnp.where(low, vrol, vcol)
            kdup = kdup.astype(BF16)
            v_lo = jnp.where(low, vdup, 0.0).astype(BF16)
            v_hi = jnp.where(low, 0.0, vdup).astype(BF16)
            for pr in range(SWA_GROUP // 2):
                p_idx = kv * (SWA_GROUP // 2) + pr
                qp = q_ref[:, p_idx * LANE:(p_idx + 1) * LANE] * (SWA_HD ** -0.5)
                acc = None
                inv = None
                for e in range(2):
                    h = 2 * p_idx + e
                    qm = jnp.where(low_q if e == 0 else ~low_q, qp, 0.0).astype(BF16)
                    logits = lax.dot_general(qm, kdup, (((1,), (1,)), ((), ())),
                                             preferred_element_type=F32)
                    logits = logits - slopes[h] * distf
                    logits = jnp.where(valid, logits, neg)
                    sink = sink_ref[h]
                    mx = jnp.maximum(jnp.max(logits, axis=-1, keepdims=True), sink)
                    p = jnp.exp(logits - mx)
                    den = jnp.sum(p, axis=-1, keepdims=True) + jnp.exp(sink - mx)
                    r = 1.0 / den
                    contrib = jnp.dot(p.astype(BF16), v_lo if e == 0 else v_hi,
                                      preferred_element_type=F32)
                    if e == 0:
                        acc, inv = contrib, r
                    else:
                        acc = acc + contrib
                        inv = jnp.where(low_q, inv, r)
                o_ref[:, p_idx * LANE:(p_idx + 1) * LANE] = (acc * inv).astype(BF16)


def _swa(proj, sinks, batch, seq):
    m = proj.shape[0]
    w = WINDOW
    nb = seq // w
    n_heads_cols = SWA_HEADS * SWA_HD
    kvw = SWA_KV_HEADS * SWA_HD
    cur = lambda b, n: b * nb + n
    prev = lambda b, n: b * nb + jnp.maximum(n - 1, 0)
    return pl.pallas_call(
        _swa_kernel,
        grid=(batch, nb),
        in_specs=[
            pl.BlockSpec(memory_space=pltpu.SMEM),
            pl.BlockSpec((w, n_heads_cols), lambda b, n: (cur(b, n), SQ_OFF // n_heads_cols)),
            pl.BlockSpec((w, kvw), lambda b, n: (prev(b, n), SK_OFF // kvw)),
            pl.BlockSpec((w, kvw), lambda b, n: (cur(b, n), SK_OFF // kvw)),
            pl.BlockSpec((w, kvw), lambda b, n: (prev(b, n), SV_OFF // kvw)),
            pl.BlockSpec((w, kvw), lambda b, n: (cur(b, n), SV_OFF // kvw)),
        ],
        out_specs=pl.BlockSpec((w, n_heads_cols), lambda b, n: (cur(b, n), 0)),
        out_shape=jax.ShapeDtypeStruct((m, n_heads_cols), BF16),
        compiler_params=pltpu.CompilerParams(
            dimension_semantics=("arbitrary", "arbitrary"),
            vmem_limit_bytes=VMEM_LIMIT),
        name="swa",
    )(sinks, proj, proj, proj, proj, proj)


def _merge_kernel(oa_ref, ob_ref, gg_ref, gs_ref, x_ref, wg_ref, ws_ref, wo_ref,
                  g_ref, b_ref, o_ref):
    ya = jnp.dot(oa_ref[...], wg_ref[...], preferred_element_type=F32)
    yb = jnp.dot(ob_ref[...], ws_ref[...], preferred_element_type=F32)
    merged = (jax.nn.sigmoid(gg_ref[...].astype(F32)) * ya
              + jax.nn.sigmoid(gs_ref[...].astype(F32)) * yb)
    mix = jnp.dot(merged.astype(BF16), wo_ref[...], preferred_element_type=F32)
    o_ref[...] = _layer_norm(ALPHA * x_ref[...] + mix, g_ref[...], b_ref[...])


def _merge(o_a, o_b, proj, x2, wg, ws, wo, ln_g, ln_b, tm=256):
    m = x2.shape[0]
    wspec = pl.BlockSpec((D_MODEL, D_MODEL), lambda i: (0, 0),
                         pipeline_mode=pl.Buffered(1))
    vspec = pl.BlockSpec((1, D_MODEL), lambda i: (0, 0))
    return pl.pallas_call(
        _merge_kernel,
        grid=(m // tm,),
        in_specs=[
            pl.BlockSpec((tm, D_MODEL), lambda i: (i, 0)),
            pl.BlockSpec((tm, D_MODEL), lambda i: (i, 0)),
            pl.BlockSpec((tm, D_MODEL), lambda i: (i, GG_OFF // D_MODEL)),
            pl.BlockSpec((tm, D_MODEL), lambda i: (i, GS_OFF // D_MODEL)),
            pl.BlockSpec((tm, D_MODEL), lambda i: (i, 0)),
            wspec, wspec, wspec, vspec, vspec,
        ],
        out_specs=pl.BlockSpec((tm, D_MODEL), lambda i: (i, 0)),
        out_shape=jax.ShapeDtypeStruct((m, D_MODEL), F32),
        compiler_params=pltpu.CompilerParams(
            dimension_semantics=("arbitrary",),
            vmem_limit_bytes=VMEM_LIMIT),
        name="merge",
    )(o_a, o_b, proj, proj, x2, wg, ws, wo, ln_g, ln_b)


def _ffn_kernel(x_ref, wu_ref, wd_ref, g_ref, b_ref, o_ref, xb_ref):
    j = pl.program_id(1)

    @pl.when(j == 0)
    def _():
        xb_ref[...] = x_ref[...].astype(BF16)

    h = jnp.dot(xb_ref[...], wu_ref[...], preferred_element_type=F32)
    h = jnp.maximum(h, 0.0)
    h = (h * h).astype(BF16)
    part = jnp.dot(h, wd_ref[...], preferred_element_type=F32)

    @pl.when(j == 0)
    def _():
        o_ref[...] = part

    @pl.when(j > 0)
    def _():
        o_ref[...] += part

    @pl.when(j == pl.num_programs(1) - 1)
    def _():
        o_ref[...] = _layer_norm(ALPHA * x_ref[...] + o_ref[...],
                                 g_ref[...], b_ref[...])


def _ffn(x1, wu, wd, ln_g, ln_b, tm=512, tf=1024):
    m = x1.shape[0]
    vspec = pl.BlockSpec((1, D_MODEL), lambda i, j: (0, 0))
    return pl.pallas_call(
        _ffn_kernel,
        grid=(m // tm, D_FF // tf),
        in_specs=[
            pl.BlockSpec((tm, D_MODEL), lambda i, j: (i, 0)),
            pl.BlockSpec((D_MODEL, tf), lambda i, j: (0, j)),
            pl.BlockSpec((tf, D_MODEL), lambda i, j: (j, 0)),
            vspec, vspec,
        ],
        out_specs=pl.BlockSpec((tm, D_MODEL), lambda i, j: (i, 0)),
        out_shape=jax.ShapeDtypeStruct((m, D_MODEL), F32),
        scratch_shapes=[pltpu.VMEM((tm, D_MODEL), BF16)],
        compiler_params=pltpu.CompilerParams(
            dimension_semantics=("arbitrary", "arbitrary"),
            vmem_limit_bytes=VMEM_LIMIT),
        name="ffn",
    )(x1, wu, wd, ln_g, ln_b)


def kernel(x, w_in, w_alpha_up, b_alpha, gla_norm_w, attn_sinks, w_branch_gla,
           w_branch_swa, w_out, ln1_g, ln1_b, w_ff_up, w_ff_down, ln2_g, ln2_b):
    batch, seq, d = x.shape
    x2 = x.reshape(batch * seq, d)

    w_main = jnp.concatenate(
        [w_in[:, 0:6144], w_in[:, 6160:8208], w_in[:, 8720:10768],
         w_in[:, 10768:12816], w_in[:, 8208:8464], w_in[:, 8464:8720]],
        axis=1).astype(BF16)
    w_lr = jnp.pad(w_in[:, 6144:6160], ((0, 0), (0, LANE - GLA_RANK))).astype(BF16)
    wup_pad = jnp.pad(w_alpha_up, ((0, LANE - GLA_RANK), (0, 0))).astype(BF16)

    proj, glr = _proj(x2, w_main, w_lr)
    o_a = _gla(proj, glr, wup_pad, b_alpha.reshape(1, -1).astype(F32),
               gla_norm_w.reshape(1, -1).astype(F32), batch, seq)
    o_b = _swa(proj, attn_sinks.astype(F32), batch, seq)
    x1 = _merge(o_a, o_b, proj, x2, w_branch_gla.astype(BF16),
                w_branch_swa.astype(BF16), w_out.astype(BF16),
                ln1_g.reshape(1, -1), ln1_b.reshape(1, -1))
    out = _ffn(x1, w_ff_up.astype(BF16), w_ff_down.astype(BF16),
               ln2_g.reshape(1, -1), ln2_b.reshape(1, -1))
    return out.reshape(batch, seq, d)
```

```python
import functools

import jax
import jax.numpy as jnp
import numpy as np
from jax import lax
from jax.experimental import pallas as pl
from jax.experimental.pallas import tpu as pltpu

F32 = jnp.float32
BF16 = jnp.bfloat16

D_MODEL = 2048
GLA_HEADS = 4
GLA_DK = 256
GLA_DV = 512
GLA_RANK = 16
GLA_TAU = 16.0
GLA_CHUNK = 64
SWA_HEADS = 32
SWA_KV_HEADS = 4
SWA_HD = 64
SWA_GROUP = 8
WINDOW = 128
D_FF = 4 * D_MODEL
ALPHA = 2.0 ** 0.25
LN_EPS = 1e-5
RMS_EPS = 1e-6

Q_OFF, K_OFF, V_OFF, GO_OFF, SQ_OFF, GG_OFF, GS_OFF, SK_OFF, SV_OFF = (
    0, 1024, 2048, 4096, 6144, 8192, 10240, 12288, 12544)
N_PROJ = 12800
LANE = 128

VMEM_LIMIT = 56 * 1024 * 1024


def _layer_norm(z, g, b):
    mu = jnp.mean(z, axis=-1, keepdims=True)
    zc = z - mu
    var = jnp.mean(zc * zc, axis=-1, keepdims=True)
    return zc * lax.rsqrt(var + LN_EPS) * g + b


def _proj_kernel(x_ref, w_ref, wlr_ref, o_ref, lr_ref, xb_ref):
    @pl.when(pl.program_id(1) == 0)
    def _():
        xb = x_ref[...].astype(BF16)
        xb_ref[...] = xb
        lr_ref[...] = jnp.dot(xb, wlr_ref[...], preferred_element_type=F32)

    o_ref[...] = jnp.dot(xb_ref[...], w_ref[...],
                         preferred_element_type=F32).astype(BF16)


def _proj(x2, w_main, w_lr, tm=1024, tn=1280):
    m = x2.shape[0]
    return pl.pallas_call(
        _proj_kernel,
        grid=(m // tm, N_PROJ // tn),
        in_specs=[
            pl.BlockSpec((tm, D_MODEL), lambda i, j: (i, 0)),
            pl.BlockSpec((D_MODEL, tn), lambda i, j: (0, j)),
            pl.BlockSpec((D_MODEL, LANE), lambda i, j: (0, 0)),
        ],
        out_specs=[
            pl.BlockSpec((tm, tn), lambda i, j: (i, j)),
            pl.BlockSpec((tm, LANE), lambda i, j: (i, 0)),
        ],
        out_shape=[
            jax.ShapeDtypeStruct((m, N_PROJ), BF16),
            jax.ShapeDtypeStruct((m, LANE), F32),
        ],
        scratch_shapes=[pltpu.VMEM((tm, D_MODEL), BF16)],
        compiler_params=pltpu.CompilerParams(
            dimension_semantics=("arbitrary", "arbitrary"),
            vmem_limit_bytes=VMEM_LIMIT),
        name="proj",
    )(x2, w_main, w_lr)


def _gla_kernel(q_ref, k_ref, v_ref, go_ref, lr_ref, wup_ref, ba_ref, nw_ref,
                o_ref, st_ref, *, t_rows):
    c = GLA_CHUNK
    n_chunks = t_rows // c
    cs = 2 * LANE

    @pl.when(pl.program_id(2) == 0)
    def _():
        st_ref[...] = jnp.zeros_like(st_ref)

    row = lax.broadcasted_iota(jnp.int32, (cs, cs), 0)
    col = lax.broadcasted_iota(jnp.int32, (cs, cs), 1)
    tri = jnp.where((row >= col) & (row // c == col // c), 1.0, 0.0).astype(BF16)
    r64 = lax.broadcasted_iota(jnp.int32, (c, c), 0)
    c64 = lax.broadcasted_iota(jnp.int32, (c, c), 1)
    causal = r64 >= c64

    z = jnp.dot(lr_ref[...].astype(BF16), wup_ref[...],
                preferred_element_type=F32) + ba_ref[...]
    la = jax.nn.log_sigmoid(z) * (1.0 / GLA_TAU)
    la_hi = la.astype(BF16)
    la_lo = (la - la_hi.astype(F32)).astype(BF16)
    b = jnp.concatenate(
        [jnp.dot(tri, la_hi[s:s + cs], preferred_element_type=F32)
         + jnp.dot(tri, la_lo[s:s + cs], preferred_element_type=F32)
         for s in range(0, t_rows, cs)], axis=0)
    b_last = [b[(i + 1) * c - 1:(i + 1) * c, :] for i in range(n_chunks)]
    bl = jnp.concatenate([jnp.broadcast_to(r, (c, GLA_DK)) for r in b_last], axis=0)
    q = q_ref[...].astype(F32) * (GLA_DK ** -0.5)
    k = k_ref[...].astype(F32)
    q_in = (q * jnp.exp(b)).astype(BF16)
    k_in = (k * jnp.exp(-b)).astype(BF16)
    k_dec = (k * jnp.exp(bl - b)).astype(BF16)

    chunks = [slice(i * c, (i + 1) * c) for i in range(n_chunks)]
    scores = [
        jnp.where(causal,
                  lax.dot_general(q_in[r], k_in[r], (((1,), (1,)), ((), ())),
                                  preferred_element_type=F32),
                  0.0).astype(BF16)
        for r in chunks]
    upds = [lax.dot_general(v_ref[r, :], k_dec[r], (((0,), (0,)), ((), ())),
                            preferred_element_type=F32)
            for r in chunks]

    st = st_ref[...]
    outs = []
    for i, r in enumerate(chunks):
        outs.append(
            jnp.dot(scores[i], v_ref[r, :], preferred_element_type=F32)
            + lax.dot_general(q_in[r], st.astype(BF16), (((1,), (1,)), ((), ())),
                              preferred_element_type=F32))
        st = st * jnp.exp(b_last[i]) + upds[i]
    st_ref[...] = st

    o = jnp.concatenate(outs, axis=0)
    rms = lax.rsqrt(jnp.mean(o * o, axis=-1, keepdims=True) + RMS_EPS)
    g = go_ref[...].astype(F32)
    o_ref[...] = (o * rms * nw_ref[...] * (g * jax.nn.sigmoid(g))).astype(BF16)


def _gla(proj, glr, wup_pad, b_alpha2, norm_w2, batch, seq, t_rows=512):
    m = proj.shape[0]
    nt = seq // t_rows
    row = lambda b, h, t: b * nt + t
    return pl.pallas_call(
        functools.partial(_gla_kernel, t_rows=t_rows),
        grid=(batch, GLA_HEADS, nt),
        in_specs=[
            pl.BlockSpec((t_rows, GLA_DK), lambda b, h, t: (row(b, h, t), Q_OFF // GLA_DK + h)),
            pl.BlockSpec((t_rows, GLA_DK), lambda b, h, t: (row(b, h, t), K_OFF // GLA_DK + h)),
            pl.BlockSpec((t_rows, GLA_DV), lambda b, h, t: (row(b, h, t), V_OFF // GLA_DV + h)),
            pl.BlockSpec((t_rows, GLA_DV), lambda b, h, t: (row(b, h, t), GO_OFF // GLA_DV + h)),
            pl.BlockSpec((t_rows, LANE), lambda b, h, t: (row(b, h, t), 0)),
            pl.BlockSpec((LANE, GLA_DK), lambda b, h, t: (0, h)),
            pl.BlockSpec((1, GLA_DK), lambda b, h, t: (0, h)),
            pl.BlockSpec((1, GLA_DV), lambda b, h, t: (0, 0)),
        ],
        out_specs=pl.BlockSpec((t_rows, GLA_DV), lambda b, h, t: (row(b, h, t), h)),
        out_shape=jax.ShapeDtypeStruct((m, GLA_HEADS * GLA_DV), BF16),
        scratch_shapes=[pltpu.VMEM((GLA_DV, GLA_DK), F32)],
        compiler_params=pltpu.CompilerParams(
            dimension_semantics=("arbitrary", "arbitrary", "arbitrary"),
            vmem_limit_bytes=VMEM_LIMIT),
        name="gla",
    )(proj, proj, proj, proj, glr, wup_pad, b_alpha2, norm_w2)


def _alibi_slopes():
    h = np.arange(1, SWA_HEADS + 1, dtype=np.float32)
    return [float(s) for s in (2.0 ** (-8.0 * h / SWA_HEADS)).astype(np.float32)]


MASKED = -1e30
ONES_ROWS = 16
SWA_LOOKAHEAD = 5


def _swa_kernel(sink_ref, slope_ref, q_ref, kp_ref, kc_ref, vp_ref, vc_ref, o_ref,
                tbl_ref):
    w = WINDOW
    n = pl.program_id(1)

    @pl.when((pl.program_id(0) == 0) & (n == 0))
    def _():
        s = lax.broadcasted_iota(jnp.int32, (2 * w, w), 0)
        t = lax.broadcasted_iota(jnp.int32, (2 * w, w), 1)
        dist = t - s + w
        band = (dist >= 0) & (dist < w)
        ndist = -dist.astype(F32)

        def fill(h, carry):
            bias = slope_ref[h] * ndist
            tbl_ref[0, h] = jnp.where(band & (s >= w), bias, MASKED)
            tbl_ref[1, h] = jnp.where(band, bias, MASKED)
            return carry

        lax.fori_loop(0, SWA_HEADS, fill, 0)

    var = jnp.minimum(n, 1)
    lane = lax.broadcasted_iota(jnp.int32, (2 * w, LANE), 1)
    low = lane < SWA_HD
    lane_q = lax.broadcasted_iota(jnp.int32, (w, LANE), 1)
    low_q = lane_q < SWA_HD

    kfull = jnp.concatenate([kp_ref[...], kc_ref[...]], axis=0)
    vfull = jnp.concatenate([vp_ref[...], vc_ref[...]], axis=0).astype(F32)
    v_t = vfull.T.astype(BF16)
    ones = jnp.ones((ONES_ROWS, 2 * w), BF16)

    kdups, vaugs = [], []
    for kvp in range(SWA_KV_HEADS // 2):
        kcol = kfull[:, kvp * LANE:(kvp + 1) * LANE].astype(F32)
        krol = pltpu.roll(kcol, SWA_HD, 1)
        kdups.append(jnp.where(low, kcol, krol).astype(BF16))
        kdups.append(jnp.where(low, krol, kcol).astype(BF16))
    for kv in range(SWA_KV_HEADS):
        vaugs.append(jnp.concatenate([v_t[kv * SWA_HD:(kv + 1) * SWA_HD, :], ones], axis=0))

    n_pairs = SWA_HEADS // 2
    pairs_per_kv = SWA_GROUP // 2

    def logits_t(p_idx):
        qp = q_ref[:, p_idx * LANE:(p_idx + 1) * LANE] * (SWA_HD ** -0.5)
        qw = jnp.concatenate([jnp.where(low_q, qp, 0.0),
                              jnp.where(low_q, 0.0, qp)], axis=0)
        return lax.dot_general(kdups[p_idx // pairs_per_kv], qw, (((1,), (1,)), ((), ())),
                               preferred_element_type=F32)

    def finish(p_idx, lg):
        ps, ms = [], []
        for e in range(2):
            h = 2 * p_idx + e
            l = lg[:, e * w:(e + 1) * w] + tbl_ref[var, h]
            mx = jnp.maximum(jnp.max(l, axis=0, keepdims=True), sink_ref[h])
            ps.append(jnp.exp(l - mx).astype(BF16))
            ms.append(mx)
        ot = jnp.dot(vaugs[p_idx // pairs_per_kv], jnp.concatenate(ps, axis=1),
                     preferred_element_type=F32)
        halves = []
        for e in range(2):
            h = 2 * p_idx + e
            den = (ot[SWA_HD:SWA_HD + 1, e * w:(e + 1) * w]
                   + jnp.exp(sink_ref[h] - ms[e]))
            halves.append(ot[0:SWA_HD, e * w:(e + 1) * w] * (1.0 / den))
        o_ref[:, p_idx * LANE:(p_idx + 1) * LANE] = (
            jnp.concatenate(halves, axis=0).T.astype(BF16))

    pending = {p: logits_t(p) for p in range(SWA_LOOKAHEAD)}
    for p in range(n_pairs):
        finish(p, pending.pop(p))
        if p + SWA_LOOKAHEAD < n_pairs:
            pending[p + SWA_LOOKAHEAD] = logits_t(p + SWA_LOOKAHEAD)


def _swa(proj, sinks, batch, seq):
    m = proj.shape[0]
    w = WINDOW
    nb = seq // w
    n_heads_cols = SWA_HEADS * SWA_HD
    kvw = SWA_KV_HEADS * SWA_HD
    cur = lambda b, n: b * nb + n
    prev = lambda b, n: b * nb + jnp.maximum(n - 1, 0)
    return pl.pallas_call(
        _swa_kernel,
        grid=(batch, nb),
        in_specs=[
            pl.BlockSpec(memory_space=pltpu.SMEM),
            pl.BlockSpec(memory_space=pltpu.SMEM),
            pl.BlockSpec((w, n_heads_cols), lambda b, n: (cur(b, n), SQ_OFF // n_heads_cols)),
            pl.BlockSpec((w, kvw), lambda b, n: (prev(b, n), SK_OFF // kvw)),
            pl.BlockSpec((w, kvw), lambda b, n: (cur(b, n), SK_OFF // kvw)),
            pl.BlockSpec((w, kvw), lambda b, n: (prev(b, n), SV_OFF // kvw)),
            pl.BlockSpec((w, kvw), lambda b, n: (cur(b, n), SV_OFF // kvw)),
        ],
        out_specs=pl.BlockSpec((w, n_heads_cols), lambda b, n: (cur(b, n), 0)),
        out_shape=jax.ShapeDtypeStruct((m, n_heads_cols), BF16),
        scratch_shapes=[pltpu.VMEM((2, SWA_HEADS, 2 * w, w), F32)],
        compiler_params=pltpu.CompilerParams(
            dimension_semantics=("arbitrary", "arbitrary"),
            vmem_limit_bytes=VMEM_LIMIT),
        name="swa",
    )(sinks, jnp.asarray(_alibi_slopes(), F32), proj, proj, proj, proj, proj)


def _merge_kernel(oa_ref, ob_ref, gg_ref, gs_ref, x_ref, wg_ref, ws_ref, wo_ref,
                  g_ref, b_ref, o_ref):
    ya = jnp.dot(oa_ref[...], wg_ref[...], preferred_element_type=F32)
    yb = jnp.dot(ob_ref[...], ws_ref[...], preferred_element_type=F32)
    merged = (jax.nn.sigmoid(gg_ref[...].astype(F32)) * ya
              + jax.nn.sigmoid(gs_ref[...].astype(F32)) * yb)
    mix = jnp.dot(merged.astype(BF16), wo_ref[...], preferred_element_type=F32)
    o_ref[...] = _layer_norm(ALPHA * x_ref[...] + mix, g_ref[...], b_ref[...])


def _merge(o_a, o_b, proj, x2, wg, ws, wo, ln_g, ln_b, tm=256):
    m = x2.shape[0]
    wspec = pl.BlockSpec((D_MODEL, D_MODEL), lambda i: (0, 0),
                         pipeline_mode=pl.Buffered(1))
    vspec = pl.BlockSpec((1, D_MODEL), lambda i: (0, 0))
    return pl.pallas_call(
        _merge_kernel,
        grid=(m // tm,),
        in_specs=[
            pl.BlockSpec((tm, D_MODEL), lambda i: (i, 0)),
            pl.BlockSpec((tm, D_MODEL), lambda i: (i, 0)),
            pl.BlockSpec((tm, D_MODEL), lambda i: (i, GG_OFF // D_MODEL)),
            pl.BlockSpec((tm, D_MODEL), lambda i: (i, GS_OFF // D_MODEL)),
            pl.BlockSpec((tm, D_MODEL), lambda i: (i, 0)),
            wspec, wspec, wspec, vspec, vspec,
        ],
        out_specs=pl.BlockSpec((tm, D_MODEL), lambda i: (i, 0)),
        out_shape=jax.ShapeDtypeStruct((m, D_MODEL), F32),
        compiler_params=pltpu.CompilerParams(
            dimension_semantics=("arbitrary",),
            vmem_limit_bytes=VMEM_LIMIT),
        name="merge",
    )(o_a, o_b, proj, proj, x2, wg, ws, wo, ln_g, ln_b)


def _ffn_kernel(x_ref, wu_ref, wd_ref, g_ref, b_ref, o_ref, xb_ref):
    j = pl.program_id(1)

    @pl.when(j == 0)
    def _():
        xb_ref[...] = x_ref[...].astype(BF16)

    h = jnp.dot(xb_ref[...], wu_ref[...], preferred_element_type=F32)
    h = jnp.maximum(h, 0.0)
    h = (h * h).astype(BF16)
    part = jnp.dot(h, wd_ref[...], preferred_element_type=F32)

    @pl.when(j == 0)
    def _():
        o_ref[...] = part

    @pl.when(j > 0)
    def _():
        o_ref[...] += part

    @pl.when(j == pl.num_programs(1) - 1)
    def _():
        o_ref[...] = _layer_norm(ALPHA * x_ref[...] + o_ref[...],
                                 g_ref[...], b_ref[...])


def _ffn(x1, wu, wd, ln_g, ln_b, tm=512, tf=1024):
    m = x1.shape[0]
    vspec = pl.BlockSpec((1, D_MODEL), lambda i, j: (0, 0))
    return pl.pallas_call(
        _ffn_kernel,
        grid=(m // tm, D_FF // tf),
        in_specs=[
            pl.BlockSpec((tm, D_MODEL), lambda i, j: (i, 0)),
            pl.BlockSpec((D_MODEL, tf), lambda i, j: (0, j)),
            pl.BlockSpec((tf, D_MODEL), lambda i, j: (j, 0)),
            vspec, vspec,
        ],
        out_specs=pl.BlockSpec((tm, D_MODEL), lambda i, j: (i, 0)),
        out_shape=jax.ShapeDtypeStruct((m, D_MODEL), F32),
        scratch_shapes=[pltpu.VMEM((tm, D_MODEL), BF16)],
        compiler_params=pltpu.CompilerParams(
            dimension_semantics=("arbitrary", "arbitrary"),
            vmem_limit_bytes=VMEM_LIMIT),
        name="ffn",
    )(x1, wu, wd, ln_g, ln_b)


def kernel(x, w_in, w_alpha_up, b_alpha, gla_norm_w, attn_sinks, w_branch_gla,
           w_branch_swa, w_out, ln1_g, ln1_b, w_ff_up, w_ff_down, ln2_g, ln2_b):
    batch, seq, d = x.shape
    x2 = x.reshape(batch * seq, d)

    w_main = jnp.concatenate(
        [w_in[:, 0:6144], w_in[:, 6160:8208], w_in[:, 8720:10768],
         w_in[:, 10768:12816], w_in[:, 8208:8464], w_in[:, 8464:8720]],
        axis=1).astype(BF16)
    w_lr = jnp.pad(w_in[:, 6144:6160], ((0, 0), (0, LANE - GLA_RANK))).astype(BF16)
    wup_pad = jnp.pad(w_alpha_up, ((0, LANE - GLA_RANK), (0, 0))).astype(BF16)

    proj, glr = _proj(x2, w_main, w_lr)
    o_a = _gla(proj, glr, wup_pad, b_alpha.reshape(1, -1).astype(F32),
               gla_norm_w.reshape(1, -1).astype(F32), batch, seq)
    o_b = _swa(proj, attn_sinks.astype(F32), batch, seq)
    x1 = _merge(o_a, o_b, proj, x2, w_branch_gla.astype(BF16),
                w_branch_swa.astype(BF16), w_out.astype(BF16),
                ln1_g.reshape(1, -1), ln1_b.reshape(1, -1))
    out = _ffn(x1, w_ff_up.astype(BF16), w_ff_down.astype(BF16),
               ln2_g.reshape(1, -1), ln2_b.reshape(1, -1))
    return out.reshape(batch, seq, d)
```

```python
import functools

import jax
import jax.numpy as jnp
import numpy as np
from jax import lax
from jax.experimental import pallas as pl
from jax.experimental.pallas import tpu as pltpu

F32 = jnp.float32
BF16 = jnp.bfloat16

D_MODEL = 2048
GLA_HEADS = 4
GLA_DK = 256
GLA_DV = 512
GLA_RANK = 16
GLA_TAU = 16.0
GLA_CHUNK = 64
SWA_HEADS = 32
SWA_KV_HEADS = 4
SWA_HD = 64
SWA_GROUP = 8
WINDOW = 128
D_FF = 4 * D_MODEL
ALPHA = 2.0 ** 0.25
LN_EPS = 1e-5
RMS_EPS = 1e-6
LOG2E = 1.4426950408889634

Q_OFF, K_OFF, V_OFF, GO_OFF, SQ_OFF, GG_OFF, GS_OFF, SK_OFF, SV_OFF = (
    0, 1024, 2048, 4096, 6144, 8192, 10240, 12288, 12544)
N_PROJ = 12800
LANE = 128

VMEM_LIMIT = 56 * 1024 * 1024


def _layer_norm(z, g, b):
    mu = jnp.mean(z, axis=-1, keepdims=True)
    zc = z - mu
    var = jnp.mean(zc * zc, axis=-1, keepdims=True)
    return zc * lax.rsqrt(var + LN_EPS) * g + b


def _proj_kernel(x_ref, w_ref, wlr_ref, o_ref, lr_ref, xb_ref):
    @pl.when(pl.program_id(1) == 0)
    def _():
        xb = x_ref[...].astype(BF16)
        xb_ref[...] = xb
        lr_ref[...] = jnp.dot(xb, wlr_ref[...], preferred_element_type=F32)

    o_ref[...] = jnp.dot(xb_ref[...], w_ref[...],
                         preferred_element_type=F32).astype(BF16)


def _proj(x2, w_main, w_lr, tm=1024, tn=1280):
    m = x2.shape[0]
    return pl.pallas_call(
        _proj_kernel,
        grid=(m // tm, N_PROJ // tn),
        in_specs=[
            pl.BlockSpec((tm, D_MODEL), lambda i, j: (i, 0)),
            pl.BlockSpec((D_MODEL, tn), lambda i, j: (0, j)),
            pl.BlockSpec((D_MODEL, LANE), lambda i, j: (0, 0)),
        ],
        out_specs=[
            pl.BlockSpec((tm, tn), lambda i, j: (i, j)),
            pl.BlockSpec((tm, LANE), lambda i, j: (i, 0)),
        ],
        out_shape=[
            jax.ShapeDtypeStruct((m, N_PROJ), BF16),
            jax.ShapeDtypeStruct((m, LANE), F32),
        ],
        scratch_shapes=[pltpu.VMEM((tm, D_MODEL), BF16)],
        compiler_params=pltpu.CompilerParams(
            dimension_semantics=("arbitrary", "arbitrary"),
            vmem_limit_bytes=VMEM_LIMIT),
        name="proj",
    )(x2, w_main, w_lr)


def _gla_kernel(q_ref, k_ref, v_ref, go_ref, lr_ref, wup_ref, ba_ref, nw_ref,
                o_ref, st_ref, *, t_rows):
    c = GLA_CHUNK
    cs = 2 * LANE

    @pl.when(pl.program_id(2) == 0)
    def _():
        st_ref[...] = jnp.zeros_like(st_ref)

    row = lax.broadcasted_iota(jnp.int32, (cs, cs), 0)
    col = lax.broadcasted_iota(jnp.int32, (cs, cs), 1)
    tri = jnp.where((row >= col) & (row // c == col // c), 1.0, 0.0).astype(BF16)
    r64 = lax.broadcasted_iota(jnp.int32, (c, c), 0)
    c64 = lax.broadcasted_iota(jnp.int32, (c, c), 1)
    causal = r64 >= c64

    half = t_rows // 2
    hc = half // c

    def stage_z(s):
        return jnp.dot(lr_ref[s:s + half, :].astype(BF16), wup_ref[...],
                       preferred_element_type=F32) + ba_ref[...]

    def stage_cumsum(z):
        softplus2 = jnp.log2(1.0 + jnp.exp2(jnp.abs(z) * (-LOG2E)))
        la = (jnp.minimum(z, 0.0) * LOG2E - softplus2) * (1.0 / GLA_TAU)
        la_hi = la.astype(BF16)
        la_lo = (la - la_hi.astype(F32)).astype(BF16)
        return jnp.concatenate(
            [jnp.dot(tri, la_hi[r:r + cs], preferred_element_type=F32)
             + jnp.dot(tri, la_lo[r:r + cs], preferred_element_type=F32)
             for r in range(0, half, cs)], axis=0)

    def stage_qk(s, b):
        b_last = [b[(i + 1) * c - 1:(i + 1) * c, :] for i in range(hc)]
        bl = jnp.concatenate([jnp.broadcast_to(r, (c, GLA_DK)) for r in b_last], axis=0)
        q = q_ref[s:s + half, :].astype(F32)
        k = k_ref[s:s + half, :].astype(F32)
        q_in = (q * jnp.exp2(b)).astype(BF16)
        k_in = (k * jnp.exp2(-b)).astype(BF16)
        k_dec = (k * jnp.exp2(bl - b)).astype(BF16)
        return q_in, k_in, k_dec, [jnp.exp2(r) for r in b_last]

    def stage_mm(s, q_in, k_in, k_dec):
        sub = [slice(i * c, (i + 1) * c) for i in range(hc)]
        scores = [
            jnp.where(causal,
                      lax.dot_general(q_in[r], k_in[r], (((1,), (1,)), ((), ())),
                                      preferred_element_type=F32),
                      0.0).astype(BF16)
            for r in sub]
        upds = [lax.dot_general(v_ref[s + r.start:s + r.stop, :], k_dec[r],
                                (((0,), (0,)), ((), ())),
                                preferred_element_type=F32)
                for r in sub]
        return scores, upds

    def stage_recur(s, st, q_in, scores, upds, decays, lo, hi):
        outs = []
        for i in range(lo, hi):
            r = slice(i * c, (i + 1) * c)
            outs.append(
                jnp.dot(scores[i], v_ref[s + r.start:s + r.stop, :],
                        preferred_element_type=F32)
                + lax.dot_general(q_in[r], st.astype(BF16), (((1,), (1,)), ((), ())),
                                  preferred_element_type=F32))
            st = st * decays[i] + upds[i]
        return st, outs

    def stage_out(s, outs):
        o = jnp.concatenate(outs, axis=0)
        rms = lax.rsqrt(jnp.mean(o * o, axis=-1, keepdims=True) + RMS_EPS * GLA_DK)
        g = go_ref[s:s + half, :].astype(F32)
        o_ref[s:s + half, :] = (o * rms * nw_ref[...]
                                * (g * jax.nn.sigmoid(g))).astype(BF16)

    sa, sb = 0, half
    st = st_ref[...]
    qa, kia, kda, da = stage_qk(sa, stage_cumsum(stage_z(sa)))
    zb = stage_z(sb)
    sca, upa = stage_mm(sa, qa, kia, kda)
    bb = stage_cumsum(zb)
    st, oa1 = stage_recur(sa, st, qa, sca, upa, da, 0, hc // 2)
    qb, kib, kdb, db = stage_qk(sb, bb)
    st, oa2 = stage_recur(sa, st, qa, sca, upa, da, hc // 2, hc)
    scb, upb = stage_mm(sb, qb, kib, kdb)
    stage_out(sa, oa1 + oa2)
    st, ob = stage_recur(sb, st, qb, scb, upb, db, 0, hc)
    stage_out(sb, ob)
    st_ref[...] = st


def _gla(proj, glr, wup_pad, b_alpha2, norm_w2, batch, seq, t_rows=1024):
    m = proj.shape[0]
    nt = seq // t_rows
    row = lambda b, h, t: b * nt + t
    return pl.pallas_call(
        functools.partial(_gla_kernel, t_rows=t_rows),
        grid=(batch, GLA_HEADS, nt),
        in_specs=[
            pl.BlockSpec((t_rows, GLA_DK), lambda b, h, t: (row(b, h, t), Q_OFF // GLA_DK + h)),
            pl.BlockSpec((t_rows, GLA_DK), lambda b, h, t: (row(b, h, t), K_OFF // GLA_DK + h)),
            pl.BlockSpec((t_rows, GLA_DV), lambda b, h, t: (row(b, h, t), V_OFF // GLA_DV + h)),
            pl.BlockSpec((t_rows, GLA_DV), lambda b, h, t: (row(b, h, t), GO_OFF // GLA_DV + h)),
            pl.BlockSpec((t_rows, LANE), lambda b, h, t: (row(b, h, t), 0)),
            pl.BlockSpec((LANE, GLA_DK), lambda b, h, t: (0, h)),
            pl.BlockSpec((1, GLA_DK), lambda b, h, t: (0, h)),
            pl.BlockSpec((1, GLA_DV), lambda b, h, t: (0, 0)),
        ],
        out_specs=pl.BlockSpec((t_rows, GLA_DV), lambda b, h, t: (row(b, h, t), h)),
        out_shape=jax.ShapeDtypeStruct((m, GLA_HEADS * GLA_DV), BF16),
        scratch_shapes=[pltpu.VMEM((GLA_DV, GLA_DK), F32)],
        compiler_params=pltpu.CompilerParams(
            dimension_semantics=("arbitrary", "arbitrary", "arbitrary"),
            vmem_limit_bytes=VMEM_LIMIT),
        name="gla",
    )(proj, proj, proj, proj, glr, wup_pad, b_alpha2, norm_w2)


def _alibi_slopes():
    h = np.arange(1, SWA_HEADS + 1, dtype=np.float32)
    return [float(s) for s in (2.0 ** (-8.0 * h / SWA_HEADS)).astype(np.float32)]


MASKED = -1e30
ONES_ROWS = 16
SWA_LOOKAHEAD = 5


def _swa_kernel(sink_ref, slope_ref, q_ref, kp_ref, kc_ref, vp_ref, vc_ref, o_ref,
                tbl_ref):
    w = WINDOW
    n = pl.program_id(1)

    @pl.when((pl.program_id(0) == 0) & (n == 0))
    def _():
        s = lax.broadcasted_iota(jnp.int32, (2 * w, w), 0)
        t = lax.broadcasted_iota(jnp.int32, (2 * w, w), 1)
        dist = t - s + w
        band = (dist >= 0) & (dist < w)
        ndist = -dist.astype(F32)

        def fill(h, carry):
            bias = (slope_ref[h] * LOG2E) * ndist
            tbl_ref[0, h] = jnp.where(band & (s >= w), bias, MASKED)
            tbl_ref[1, h] = jnp.where(band, bias, MASKED)
            return carry

        lax.fori_loop(0, SWA_HEADS, fill, 0)

    var = jnp.minimum(n, 1)
    lane = lax.broadcasted_iota(jnp.int32, (2 * w, LANE), 1)
    low = lane < SWA_HD
    lane_q = lax.broadcasted_iota(jnp.int32, (w, LANE), 1)
    low_q = lane_q < SWA_HD

    kfull = jnp.concatenate([kp_ref[...], kc_ref[...]], axis=0)
    vfull = jnp.concatenate([vp_ref[...], vc_ref[...]], axis=0).astype(F32)
    v_t = vfull.T.astype(BF16)
    ones = jnp.ones((ONES_ROWS, 2 * w), BF16)

    kdups, vaugs = [], []
    for kvp in range(SWA_KV_HEADS // 2):
        kcol = kfull[:, kvp * LANE:(kvp + 1) * LANE].astype(F32) * LOG2E
        krol = pltpu.roll(kcol, SWA_HD, 1)
        kdups.append(jnp.where(low, kcol, krol).astype(BF16))
        kdups.append(jnp.where(low, krol, kcol).astype(BF16))
    for kv in range(SWA_KV_HEADS):
        vaugs.append(jnp.concatenate([v_t[kv * SWA_HD:(kv + 1) * SWA_HD, :], ones], axis=0))

    n_pairs = SWA_HEADS // 2
    pairs_per_kv = SWA_GROUP // 2

    def logits_t(p_idx):
        qp = q_ref[:, p_idx * LANE:(p_idx + 1) * LANE] * (SWA_HD ** -0.5)
        qw = jnp.concatenate([jnp.where(low_q, qp, 0.0),
                              jnp.where(low_q, 0.0, qp)], axis=0)
        return lax.dot_general(kdups[p_idx // pairs_per_kv], qw, (((1,), (1,)), ((), ())),
                               preferred_element_type=F32)

    def finish(p_idx, lg):
        ps, ms = [], []
        for e in range(2):
            h = 2 * p_idx + e
            l = lg[:, e * w:(e + 1) * w] + tbl_ref[var, h]
            mx = jnp.maximum(jnp.max(l, axis=0, keepdims=True), sink_ref[h] * LOG2E)
            ps.append(jnp.exp2(l - mx).astype(BF16))
            ms.append(mx)
        ot = jnp.dot(vaugs[p_idx // pairs_per_kv], jnp.concatenate(ps, axis=1),
                     preferred_element_type=F32)
        halves = []
        for e in range(2):
            h = 2 * p_idx + e
            den = (ot[SWA_HD:SWA_HD + 1, e * w:(e + 1) * w]
                   + jnp.exp2(sink_ref[h] * LOG2E - ms[e]))
            halves.append(ot[0:SWA_HD, e * w:(e + 1) * w] * (1.0 / den))
        o_ref[:, p_idx * LANE:(p_idx + 1) * LANE] = (
            jnp.concatenate(halves, axis=0).T.astype(BF16))

    pending = {p: logits_t(p) for p in range(SWA_LOOKAHEAD)}
    for p in range(n_pairs):
        finish(p, pending.pop(p))
        if p + SWA_LOOKAHEAD < n_pairs:
            pending[p + SWA_LOOKAHEAD] = logits_t(p + SWA_LOOKAHEAD)


def _swa(proj, sinks, batch, seq):
    m = proj.shape[0]
    w = WINDOW
    nb = seq // w
    n_heads_cols = SWA_HEADS * SWA_HD
    kvw = SWA_KV_HEADS * SWA_HD
    cur = lambda b, n: b * nb + n
    prev = lambda b, n: b * nb + jnp.maximum(n - 1, 0)
    return pl.pallas_call(
        _swa_kernel,
        grid=(batch, nb),
        in_specs=[
            pl.BlockSpec(memory_space=pltpu.SMEM),
            pl.BlockSpec(memory_space=pltpu.SMEM),
            pl.BlockSpec((w, n_heads_cols), lambda b, n: (cur(b, n), SQ_OFF // n_heads_cols)),
            pl.BlockSpec((w, kvw), lambda b, n: (prev(b, n), SK_OFF // kvw)),
            pl.BlockSpec((w, kvw), lambda b, n: (cur(b, n), SK_OFF // kvw)),
            pl.BlockSpec((w, kvw), lambda b, n: (prev(b, n), SV_OFF // kvw)),
            pl.BlockSpec((w, kvw), lambda b, n: (cur(b, n), SV_OFF // kvw)),
        ],
        out_specs=pl.BlockSpec((w, n_heads_cols), lambda b, n: (cur(b, n), 0)),
        out_shape=jax.ShapeDtypeStruct((m, n_heads_cols), BF16),
        scratch_shapes=[pltpu.VMEM((2, SWA_HEADS, 2 * w, w), F32)],
        compiler_params=pltpu.CompilerParams(
            dimension_semantics=("arbitrary", "arbitrary"),
            vmem_limit_bytes=VMEM_LIMIT),
        name="swa",
    )(sinks, jnp.asarray(_alibi_slopes(), F32), proj, proj, proj, proj, proj)


def _merge_kernel(oa_ref, ob_ref, gg_ref, gs_ref, x_ref, wg_ref, ws_ref, wo_ref,
                  g_ref, b_ref, o_ref):
    ya = jnp.dot(oa_ref[...], wg_ref[...], preferred_element_type=F32)
    yb = jnp.dot(ob_ref[...], ws_ref[...], preferred_element_type=F32)
    merged = (jax.nn.sigmoid(gg_ref[...].astype(F32)) * ya
              + jax.nn.sigmoid(gs_ref[...].astype(F32)) * yb)
    mix = jnp.dot(merged.astype(BF16), wo_ref[...], preferred_element_type=F32)
    o_ref[...] = _layer_norm(ALPHA * x_ref[...] + mix, g_ref[...], b_ref[...])


def _merge(o_a, o_b, proj, x2, wg, ws, wo, ln_g, ln_b, tm=256):
    m = x2.shape[0]
    wspec = pl.BlockSpec((D_MODEL, D_MODEL), lambda i: (0, 0),
                         pipeline_mode=pl.Buffered(1))
    vspec = pl.BlockSpec((1, D_MODEL), lambda i: (0, 0))
    return pl.pallas_call(
        _merge_kernel,
        grid=(m // tm,),
        in_specs=[
            pl.BlockSpec((tm, D_MODEL), lambda i: (i, 0)),
            pl.BlockSpec((tm, D_MODEL), lambda i: (i, 0)),
            pl.BlockSpec((tm, D_MODEL), lambda i: (i, GG_OFF // D_MODEL)),
            pl.BlockSpec((tm, D_MODEL), lambda i: (i, GS_OFF // D_MODEL)),
            pl.BlockSpec((tm, D_MODEL), lambda i: (i, 0)),
            wspec, wspec, wspec, vspec, vspec,
        ],
        out_specs=pl.BlockSpec((tm, D_MODEL), lambda i: (i, 0)),
        out_shape=jax.ShapeDtypeStruct((m, D_MODEL), F32),
        compiler_params=pltpu.CompilerParams(
            dimension_semantics=("arbitrary",),
            vmem_limit_bytes=VMEM_LIMIT),
        name="merge",
    )(o_a, o_b, proj, proj, x2, wg, ws, wo, ln_g, ln_b)


def _ffn_kernel(x_ref, wu_ref, wd_ref, g_ref, b_ref, o_ref, xb_ref):
    j = pl.program_id(1)

    @pl.when(j == 0)
    def _():
        xb_ref[...] = x_ref[...].astype(BF16)
        o_ref[...] = jnp.zeros_like(o_ref)

    h = jnp.dot(xb_ref[...], wu_ref[...], preferred_element_type=F32)
    h = jnp.maximum(h, 0.0)
    h = (h * h).astype(BF16)
    o_ref[...] += jnp.dot(h, wd_ref[...], preferred_element_type=F32)

    @pl.when(j == pl.num_programs(1) - 1)
    def _():
        o_ref[...] = _layer_norm(ALPHA * x_ref[...] + o_ref[...],
                                 g_ref[...], b_ref[...])


def _ffn(x1, wu, wd, ln_g, ln_b, tm=512, tf=1024):
    m = x1.shape[0]
    vspec = pl.BlockSpec((1, D_MODEL), lambda i, j: (0, 0))
    return pl.pallas_call(
        _ffn_kernel,
        grid=(m // tm, D_FF // tf),
        in_specs=[
            pl.BlockSpec((tm, D_MODEL), lambda i, j: (i, 0)),
            pl.BlockSpec((D_MODEL, tf), lambda i, j: (0, j)),
            pl.BlockSpec((tf, D_MODEL), lambda i, j: (j, 0)),
            vspec, vspec,
        ],
        out_specs=pl.BlockSpec((tm, D_MODEL), lambda i, j: (i, 0)),
        out_shape=jax.ShapeDtypeStruct((m, D_MODEL), F32),
        scratch_shapes=[pltpu.VMEM((tm, D_MODEL), BF16)],
        compiler_params=pltpu.CompilerParams(
            dimension_semantics=("arbitrary", "arbitrary"),
            vmem_limit_bytes=VMEM_LIMIT),
        name="ffn",
    )(x1, wu, wd, ln_g, ln_b)


def kernel(x, w_in, w_alpha_up, b_alpha, gla_norm_w, attn_sinks, w_branch_gla,
           w_branch_swa, w_out, ln1_g, ln1_b, w_ff_up, w_ff_down, ln2_g, ln2_b):
    batch, seq, d = x.shape
    x2 = x.reshape(batch * seq, d)

    w_main = jnp.concatenate(
        [w_in[:, 0:6144], w_in[:, 6160:8208], w_in[:, 8720:10768],
         w_in[:, 10768:12816], w_in[:, 8208:8464], w_in[:, 8464:8720]],
        axis=1).astype(BF16)
    w_lr = jnp.pad(w_in[:, 6144:6160], ((0, 0), (0, LANE - GLA_RANK))).astype(BF16)
    wup_pad = jnp.pad(w_alpha_up, ((0, LANE - GLA_RANK), (0, 0))).astype(BF16)

    proj, glr = _proj(x2, w_main, w_lr)
    o_a = _gla(proj, glr, wup_pad, b_alpha.reshape(1, -1).astype(F32),
               gla_norm_w.reshape(1, -1).astype(F32), batch, seq)
    o_b = _swa(proj, attn_sinks.astype(F32), batch, seq)
    x1 = _merge(o_a, o_b, proj, x2, w_branch_gla.astype(BF16),
                w_branch_swa.astype(BF16), w_out.astype(BF16),
                ln1_g.reshape(1, -1), ln1_b.reshape(1, -1))
    out = _ffn(x1, w_ff_up.astype(BF16), w_ff_down.astype(BF16),
               ln2_g.reshape(1, -1), ln2_b.reshape(1, -1))
    return out.reshape(batch, seq, d)
```

```python
import functools

import jax
import jax.numpy as jnp
import numpy as np
from jax import lax
from jax.experimental import pallas as pl
from jax.experimental.pallas import tpu as pltpu

F32 = jnp.float32
BF16 = jnp.bfloat16

D_MODEL = 2048
GLA_HEADS = 4
GLA_DK = 256
GLA_DV = 512
GLA_RANK = 16
GLA_TAU = 16.0
GLA_CHUNK = 64
SWA_HEADS = 32
SWA_KV_HEADS = 4
SWA_HD = 64
SWA_GROUP = 8
WINDOW = 128
D_FF = 4 * D_MODEL
ALPHA = 2.0 ** 0.25
LN_EPS = 1e-5
RMS_EPS = 1e-6
LOG2E = 1.4426950408889634

Q_OFF, K_OFF, V_OFF, GO_OFF, SQ_OFF, GG_OFF, GS_OFF, SK_OFF, SV_OFF = (
    0, 1024, 2048, 4096, 6144, 8192, 10240, 12288, 12544)
N_PROJ = 12800
LANE = 128

VMEM_LIMIT = 56 * 1024 * 1024


def _layer_norm(z, g, b):
    mu = jnp.mean(z, axis=-1, keepdims=True)
    zc = z - mu
    var = jnp.mean(zc * zc, axis=-1, keepdims=True)
    return zc * lax.rsqrt(var + LN_EPS) * g + b


def _proj_kernel(x_ref, w_ref, wlr_ref, o_ref, lr_ref, xb_ref):
    @pl.when(pl.program_id(1) == 0)
    def _():
        xb = x_ref[...].astype(BF16)
        xb_ref[...] = xb
        lr_ref[...] = jnp.dot(xb, wlr_ref[...], preferred_element_type=F32)

    o_ref[...] = jnp.dot(xb_ref[...], w_ref[...],
                         preferred_element_type=F32).astype(BF16)


def _proj(x2, w_main, w_lr, tm=1024, tn=2560):
    m = x2.shape[0]
    return pl.pallas_call(
        _proj_kernel,
        grid=(m // tm, N_PROJ // tn),
        in_specs=[
            pl.BlockSpec((tm, D_MODEL), lambda i, j: (i, 0)),
            pl.BlockSpec((D_MODEL, tn), lambda i, j: (0, j)),
            pl.BlockSpec((D_MODEL, LANE), lambda i, j: (0, 0)),
        ],
        out_specs=[
            pl.BlockSpec((tm, tn), lambda i, j: (i, j)),
            pl.BlockSpec((tm, LANE), lambda i, j: (i, 0)),
        ],
        out_shape=[
            jax.ShapeDtypeStruct((m, N_PROJ), BF16),
            jax.ShapeDtypeStruct((m, LANE), F32),
        ],
        scratch_shapes=[pltpu.VMEM((tm, D_MODEL), BF16)],
        compiler_params=pltpu.CompilerParams(
            dimension_semantics=("arbitrary", "arbitrary"),
            vmem_limit_bytes=VMEM_LIMIT),
        name="proj",
    )(x2, w_main, w_lr)


def _gla_kernel(q_ref, k_ref, v_ref, go_ref, lr_ref, wup_ref, ba_ref, nw_ref,
                o_ref, st_ref, *, t_rows):
    c = GLA_CHUNK
    cs = 2 * LANE

    @pl.when(pl.program_id(2) == 0)
    def _():
        st_ref[...] = jnp.zeros_like(st_ref)

    row = lax.broadcasted_iota(jnp.int32, (cs, cs), 0)
    col = lax.broadcasted_iota(jnp.int32, (cs, cs), 1)
    tri = jnp.where((row >= col) & (row // c == col // c), 1.0, 0.0).astype(BF16)
    r64 = lax.broadcasted_iota(jnp.int32, (c, c), 0)
    c64 = lax.broadcasted_iota(jnp.int32, (c, c), 1)
    causal = r64 >= c64

    half = t_rows // 2
    hc = half // c

    def stage_z(s):
        return jnp.dot(lr_ref[s:s + half, :].astype(BF16), wup_ref[...],
                       preferred_element_type=F32) + ba_ref[...]

    def stage_cumsum(z):
        softplus2 = jnp.log2(1.0 + jnp.exp2(jnp.abs(z) * (-LOG2E)))
        la = (jnp.minimum(z, 0.0) * LOG2E - softplus2) * (1.0 / GLA_TAU)
        la_hi = la.astype(BF16)
        la_lo = (la - la_hi.astype(F32)).astype(BF16)
        return jnp.concatenate(
            [jnp.dot(tri, la_hi[r:r + cs], preferred_element_type=F32)
             + jnp.dot(tri, la_lo[r:r + cs], preferred_element_type=F32)
             for r in range(0, half, cs)], axis=0)

    def stage_qk(s, b):
        b_last = [b[(i + 1) * c - 1:(i + 1) * c, :] for i in range(hc)]
        bl = jnp.concatenate([jnp.broadcast_to(r, (c, GLA_DK)) for r in b_last], axis=0)
        q = q_ref[s:s + half, :].astype(F32)
        k = k_ref[s:s + half, :].astype(F32)
        q_in = (q * jnp.exp2(b)).astype(BF16)
        k_in = (k * jnp.exp2(-b)).astype(BF16)
        k_dec = (k * jnp.exp2(bl - b)).astype(BF16)
        return q_in, k_in, k_dec, [jnp.exp2(r) for r in b_last]

    def stage_mm(s, q_in, k_in, k_dec):
        sub = [slice(i * c, (i + 1) * c) for i in range(hc)]
        scores = [
            jnp.where(causal,
                      lax.dot_general(q_in[r], k_in[r], (((1,), (1,)), ((), ())),
                                      preferred_element_type=F32),
                      0.0).astype(BF16)
            for r in sub]
        upds = [lax.dot_general(v_ref[s + r.start:s + r.stop, :], k_dec[r],
                                (((0,), (0,)), ((), ())),
                                preferred_element_type=F32)
                for r in sub]
        return scores, upds

    def stage_recur(s, st, q_in, scores, upds, decays, lo, hi):
        outs = []
        for i in range(lo, hi):
            r = slice(i * c, (i + 1) * c)
            outs.append(
                jnp.dot(scores[i], v_ref[s + r.start:s + r.stop, :],
                        preferred_element_type=F32)
                + lax.dot_general(q_in[r], st.astype(BF16), (((1,), (1,)), ((), ())),
                                  preferred_element_type=F32))
            st = st * decays[i] + upds[i]
        return st, outs

    def stage_out(s, outs):
        o = jnp.concatenate(outs, axis=0)
        rms = lax.rsqrt(jnp.mean(o * o, axis=-1, keepdims=True) + RMS_EPS * GLA_DK)
        g = go_ref[s:s + half, :].astype(F32)
        o_ref[s:s + half, :] = (o * rms * nw_ref[...]
                                * (g * jax.nn.sigmoid(g))).astype(BF16)

    sa, sb = 0, half
    st = st_ref[...]
    qa, kia, kda, da = stage_qk(sa, stage_cumsum(stage_z(sa)))
    zb = stage_z(sb)
    sca, upa = stage_mm(sa, qa, kia, kda)
    bb = stage_cumsum(zb)
    st, oa1 = stage_recur(sa, st, qa, sca, upa, da, 0, hc // 2)
    qb, kib, kdb, db = stage_qk(sb, bb)
    st, oa2 = stage_recur(sa, st, qa, sca, upa, da, hc // 2, hc)
    scb, upb = stage_mm(sb, qb, kib, kdb)
    stage_out(sa, oa1 + oa2)
    st, ob = stage_recur(sb, st, qb, scb, upb, db, 0, hc)
    stage_out(sb, ob)
    st_ref[...] = st


def _gla(proj, glr, wup_pad, b_alpha2, norm_w2, batch, seq, t_rows=1024):
    m = proj.shape[0]
    nt = seq // t_rows
    row = lambda b, h, t: b * nt + t
    return pl.pallas_call(
        functools.partial(_gla_kernel, t_rows=t_rows),
        grid=(batch, GLA_HEADS, nt),
        in_specs=[
            pl.BlockSpec((t_rows, GLA_DK), lambda b, h, t: (row(b, h, t), Q_OFF // GLA_DK + h)),
            pl.BlockSpec((t_rows, GLA_DK), lambda b, h, t: (row(b, h, t), K_OFF // GLA_DK + h)),
            pl.BlockSpec((t_rows, GLA_DV), lambda b, h, t: (row(b, h, t), V_OFF // GLA_DV + h)),
            pl.BlockSpec((t_rows, GLA_DV), lambda b, h, t: (row(b, h, t), GO_OFF // GLA_DV + h)),
            pl.BlockSpec((t_rows, LANE), lambda b, h, t: (row(b, h, t), 0)),
            pl.BlockSpec((LANE, GLA_DK), lambda b, h, t: (0, h)),
            pl.BlockSpec((1, GLA_DK), lambda b, h, t: (0, h)),
            pl.BlockSpec((1, GLA_DV), lambda b, h, t: (0, 0)),
        ],
        out_specs=pl.BlockSpec((t_rows, GLA_DV), lambda b, h, t: (row(b, h, t), h)),
        out_shape=jax.ShapeDtypeStruct((m, GLA_HEADS * GLA_DV), BF16),
        scratch_shapes=[pltpu.VMEM((GLA_DV, GLA_DK), F32)],
        compiler_params=pltpu.CompilerParams(
            dimension_semantics=("arbitrary", "arbitrary", "arbitrary"),
            vmem_limit_bytes=VMEM_LIMIT),
        name="gla",
    )(proj, proj, proj, proj, glr, wup_pad, b_alpha2, norm_w2)


def _alibi_slopes():
    h = np.arange(1, SWA_HEADS + 1, dtype=np.float32)
    return [float(s) for s in (2.0 ** (-8.0 * h / SWA_HEADS)).astype(np.float32)]


MASKED = -1e30
ONES_ROWS = 16
SWA_LOOKAHEAD = 5


def _swa_kernel(sink_ref, slope_ref, q_ref, kp_ref, kc_ref, vp_ref, vc_ref, o_ref,
                tbl_ref):
    w = WINDOW
    n = pl.program_id(1)

    @pl.when((pl.program_id(0) == 0) & (n == 0))
    def _():
        s = lax.broadcasted_iota(jnp.int32, (2 * w, w), 0)
        t = lax.broadcasted_iota(jnp.int32, (2 * w, w), 1)
        dist = t - s + w
        band = (dist >= 0) & (dist < w)
        ndist = -dist.astype(F32)

        def fill(h, carry):
            bias = (slope_ref[h] * LOG2E) * ndist
            tbl_ref[0, h] = jnp.where(band & (s >= w), bias, MASKED)
            tbl_ref[1, h] = jnp.where(band, bias, MASKED)
            return carry

        lax.fori_loop(0, SWA_HEADS, fill, 0)

    var = jnp.minimum(n, 1)
    lane = lax.broadcasted_iota(jnp.int32, (2 * w, LANE), 1)
    low = lane < SWA_HD
    lane_q = lax.broadcasted_iota(jnp.int32, (w, LANE), 1)
    low_q = lane_q < SWA_HD

    kfull = jnp.concatenate([kp_ref[...], kc_ref[...]], axis=0)
    vfull = jnp.concatenate([vp_ref[...], vc_ref[...]], axis=0).astype(F32)
    v_t = vfull.T.astype(BF16)
    ones = jnp.ones((ONES_ROWS, 2 * w), BF16)

    kdups, vaugs = [], []
    for kvp in range(SWA_KV_HEADS // 2):
        kcol = kfull[:, kvp * LANE:(kvp + 1) * LANE].astype(F32) * LOG2E
        krol = pltpu.roll(kcol, SWA_HD, 1)
        kdups.append(jnp.where(low, kcol, krol).astype(BF16))
        kdups.append(jnp.where(low, krol, kcol).astype(BF16))
    for kv in range(SWA_KV_HEADS):
        vaugs.append(jnp.concatenate([v_t[kv * SWA_HD:(kv + 1) * SWA_HD, :], ones], axis=0))

    n_pairs = SWA_HEADS // 2
    pairs_per_kv = SWA_GROUP // 2

    def logits_t(p_idx):
        qp = q_ref[:, p_idx * LANE:(p_idx + 1) * LANE] * (SWA_HD ** -0.5)
        qw = jnp.concatenate([jnp.where(low_q, qp, 0.0),
                              jnp.where(low_q, 0.0, qp)], axis=0)
        return lax.dot_general(kdups[p_idx // pairs_per_kv], qw, (((1,), (1,)), ((), ())),
                               preferred_element_type=F32)

    def finish(p_idx, lg):
        ps, ms = [], []
        for e in range(2):
            h = 2 * p_idx + e
            l = lg[:, e * w:(e + 1) * w] + tbl_ref[var, h]
            mx = jnp.maximum(jnp.max(l, axis=0, keepdims=True), sink_ref[h] * LOG2E)
            ps.append(jnp.exp2(l - mx).astype(BF16))
            ms.append(mx)
        ot = jnp.dot(vaugs[p_idx // pairs_per_kv], jnp.concatenate(ps, axis=1),
                     preferred_element_type=F32)
        halves = []
        for e in range(2):
            h = 2 * p_idx + e
            den = (ot[SWA_HD:SWA_HD + 1, e * w:(e + 1) * w]
                   + jnp.exp2(sink_ref[h] * LOG2E - ms[e]))
            halves.append(ot[0:SWA_HD, e * w:(e + 1) * w] * (1.0 / den))
        o_ref[:, p_idx * LANE:(p_idx + 1) * LANE] = (
            jnp.concatenate(halves, axis=0).T.astype(BF16))

    pending = {p: logits_t(p) for p in range(SWA_LOOKAHEAD)}
    for p in range(n_pairs):
        finish(p, pending.pop(p))
        if p + SWA_LOOKAHEAD < n_pairs:
            pending[p + SWA_LOOKAHEAD] = logits_t(p + SWA_LOOKAHEAD)


def _swa(proj, sinks, batch, seq):
    m = proj.shape[0]
    w = WINDOW
    nb = seq // w
    n_heads_cols = SWA_HEADS * SWA_HD
    kvw = SWA_KV_HEADS * SWA_HD
    cur = lambda b, n: b * nb + n
    prev = lambda b, n: b * nb + jnp.maximum(n - 1, 0)
    return pl.pallas_call(
        _swa_kernel,
        grid=(batch, nb),
        in_specs=[
            pl.BlockSpec(memory_space=pltpu.SMEM),
            pl.BlockSpec(memory_space=pltpu.SMEM),
            pl.BlockSpec((w, n_heads_cols), lambda b, n: (cur(b, n), SQ_OFF // n_heads_cols)),
            pl.BlockSpec((w, kvw), lambda b, n: (prev(b, n), SK_OFF // kvw)),
            pl.BlockSpec((w, kvw), lambda b, n: (cur(b, n), SK_OFF // kvw)),
            pl.BlockSpec((w, kvw), lambda b, n: (prev(b, n), SV_OFF // kvw)),
            pl.BlockSpec((w, kvw), lambda b, n: (cur(b, n), SV_OFF // kvw)),
        ],
        out_specs=pl.BlockSpec((w, n_heads_cols), lambda b, n: (cur(b, n), 0)),
        out_shape=jax.ShapeDtypeStruct((m, n_heads_cols), BF16),
        scratch_shapes=[pltpu.VMEM((2, SWA_HEADS, 2 * w, w), F32)],
        compiler_params=pltpu.CompilerParams(
            dimension_semantics=("arbitrary", "arbitrary"),
            vmem_limit_bytes=VMEM_LIMIT),
        name="swa",
    )(sinks, jnp.asarray(_alibi_slopes(), F32), proj, proj, proj, proj, proj)


def _merge_kernel(oa_ref, ob_ref, gg_ref, gs_ref, x_ref, wg_ref, ws_ref, wo_ref,
                  g_ref, b_ref, o_ref):
    ya = jnp.dot(oa_ref[...], wg_ref[...], preferred_element_type=F32)
    yb = jnp.dot(ob_ref[...], ws_ref[...], preferred_element_type=F32)
    merged = (jax.nn.sigmoid(gg_ref[...].astype(F32)) * ya
              + jax.nn.sigmoid(gs_ref[...].astype(F32)) * yb)
    mix = jnp.dot(merged.astype(BF16), wo_ref[...], preferred_element_type=F32)
    o_ref[...] = _layer_norm(ALPHA * x_ref[...] + mix, g_ref[...], b_ref[...])


def _merge(o_a, o_b, proj, x2, wg, ws, wo, ln_g, ln_b, tm=256):
    m = x2.shape[0]
    wspec = pl.BlockSpec((D_MODEL, D_MODEL), lambda i: (0, 0),
                         pipeline_mode=pl.Buffered(1))
    vspec = pl.BlockSpec((1, D_MODEL), lambda i: (0, 0))
    return pl.pallas_call(
        _merge_kernel,
        grid=(m // tm,),
        in_specs=[
            pl.BlockSpec((tm, D_MODEL), lambda i: (i, 0)),
            pl.BlockSpec((tm, D_MODEL), lambda i: (i, 0)),
            pl.BlockSpec((tm, D_MODEL), lambda i: (i, GG_OFF // D_MODEL)),
            pl.BlockSpec((tm, D_MODEL), lambda i: (i, GS_OFF // D_MODEL)),
            pl.BlockSpec((tm, D_MODEL), lambda i: (i, 0)),
            wspec, wspec, wspec, vspec, vspec,
        ],
        out_specs=pl.BlockSpec((tm, D_MODEL), lambda i: (i, 0)),
        out_shape=jax.ShapeDtypeStruct((m, D_MODEL), F32),
        compiler_params=pltpu.CompilerParams(
            dimension_semantics=("arbitrary",),
            vmem_limit_bytes=VMEM_LIMIT),
        name="merge",
    )(o_a, o_b, proj, proj, x2, wg, ws, wo, ln_g, ln_b)


def _ffn_kernel(x_ref, wu_ref, wd_ref, g_ref, b_ref, o_ref, xb_ref):
    j = pl.program_id(1)

    @pl.when(j == 0)
    def _():
        xb_ref[...] = x_ref[...].astype(BF16)
        o_ref[...] = jnp.zeros_like(o_ref)

    h = jnp.dot(xb_ref[...], wu_ref[...], preferred_element_type=F32)
    h = jnp.maximum(h, 0.0)
    h = (h * h).astype(BF16)
    o_ref[...] += jnp.dot(h, wd_ref[...], preferred_element_type=F32)

    @pl.when(j == pl.num_programs(1) - 1)
    def _():
        o_ref[...] = _layer_norm(ALPHA * x_ref[...] + o_ref[...],
                                 g_ref[...], b_ref[...])


def _ffn(x1, wu, wd, ln_g, ln_b, tm=512, tf=1024):
    m = x1.shape[0]
    vspec = pl.BlockSpec((1, D_MODEL), lambda i, j: (0, 0))
    return pl.pallas_call(
        _ffn_kernel,
        grid=(m // tm, D_FF // tf),
        in_specs=[
            pl.BlockSpec((tm, D_MODEL), lambda i, j: (i, 0)),
            pl.BlockSpec((D_MODEL, tf), lambda i, j: (0, j)),
            pl.BlockSpec((tf, D_MODEL), lambda i, j: (j, 0)),
            vspec, vspec,
        ],
        out_specs=pl.BlockSpec((tm, D_MODEL), lambda i, j: (i, 0)),
        out_shape=jax.ShapeDtypeStruct((m, D_MODEL), F32),
        scratch_shapes=[pltpu.VMEM((tm, D_MODEL), BF16)],
        compiler_params=pltpu.CompilerParams(
            dimension_semantics=("arbitrary", "arbitrary"),
            vmem_limit_bytes=VMEM_LIMIT),
        name="ffn",
    )(x1, wu, wd, ln_g, ln_b)


def kernel(x, w_in, w_alpha_up, b_alpha, gla_norm_w, attn_sinks, w_branch_gla,
           w_branch_swa, w_out, ln1_g, ln1_b, w_ff_up, w_ff_down, ln2_g, ln2_b):
    batch, seq, d = x.shape
    x2 = x.reshape(batch * seq, d)

    w_main = jnp.concatenate(
        [w_in[:, 0:6144], w_in[:, 6160:8208], w_in[:, 8720:10768],
         w_in[:, 10768:12816], w_in[:, 8208:8464], w_in[:, 8464:8720]],
        axis=1).astype(BF16)
    w_lr = jnp.pad(w_in[:, 6144:6160], ((0, 0), (0, LANE - GLA_RANK))).astype(BF16)
    wup_pad = jnp.pad(w_alpha_up, ((0, LANE - GLA_RANK), (0, 0))).astype(BF16)

    proj, glr = _proj(x2, w_main, w_lr)
    o_a = _gla(proj, glr, wup_pad, b_alpha.reshape(1, -1).astype(F32),
               gla_norm_w.reshape(1, -1).astype(F32), batch, seq)
    o_b = _swa(proj, attn_sinks.astype(F32), batch, seq)
    x1 = _merge(o_a, o_b, proj, x2, w_branch_gla.astype(BF16),
                w_branch_swa.astype(BF16), w_out.astype(BF16),
                ln1_g.reshape(1, -1), ln1_b.reshape(1, -1))
    out = _ffn(x1, w_ff_up.astype(BF16), w_ff_down.astype(BF16),
               ln2_g.reshape(1, -1), ln2_b.reshape(1, -1))
    return out.reshape(batch, seq, d)
```

```python
import functools

import jax
import jax.numpy as jnp
import numpy as np
from jax import lax
from jax.experimental import pallas as pl
from jax.experimental.pallas import tpu as pltpu

F32 = jnp.float32
BF16 = jnp.bfloat16

D_MODEL = 2048
GLA_HEADS = 4
GLA_DK = 256
GLA_DV = 512
GLA_RANK = 16
GLA_TAU = 16.0
GLA_CHUNK = 64
SWA_HEADS = 32
SWA_KV_HEADS = 4
SWA_HD = 64
SWA_GROUP = 8
WINDOW = 128
D_FF = 4 * D_MODEL
ALPHA = 2.0 ** 0.25
LN_EPS = 1e-5
RMS_EPS = 1e-6
LOG2E = 1.4426950408889634

Q_OFF, K_OFF, V_OFF, GO_OFF, SQ_OFF, GG_OFF, GS_OFF, SK_OFF, SV_OFF = (
    0, 1024, 2048, 4096, 6144, 8192, 10240, 12288, 12544)
N_PROJ = 12800
LANE = 128

VMEM_LIMIT = 56 * 1024 * 1024


def _layer_norm(z, g, b):
    mu = jnp.mean(z, axis=-1, keepdims=True)
    zc = z - mu
    var = jnp.mean(zc * zc, axis=-1, keepdims=True)
    return zc * lax.rsqrt(var + LN_EPS) * g + b


def _proj_kernel(x_ref, w_ref, wlr_ref, o_ref, lr_ref, xb_ref):
    @pl.when(pl.program_id(1) == 0)
    def _():
        xb = x_ref[...].astype(BF16)
        xb_ref[...] = xb
        lr_ref[...] = jnp.dot(xb, wlr_ref[...], preferred_element_type=F32)

    o_ref[...] = jnp.dot(xb_ref[...], w_ref[...],
                         preferred_element_type=F32).astype(BF16)


def _proj(x2, w_main, w_lr, tm=1024, tn=2560):
    m = x2.shape[0]
    return pl.pallas_call(
        _proj_kernel,
        grid=(m // tm, N_PROJ // tn),
        in_specs=[
            pl.BlockSpec((tm, D_MODEL), lambda i, j: (i, 0)),
            pl.BlockSpec((D_MODEL, tn), lambda i, j: (0, j)),
            pl.BlockSpec((D_MODEL, LANE), lambda i, j: (0, 0)),
        ],
        out_specs=[
            pl.BlockSpec((tm, tn), lambda i, j: (i, j)),
            pl.BlockSpec((tm, LANE), lambda i, j: (i, 0)),
        ],
        out_shape=[
            jax.ShapeDtypeStruct((m, N_PROJ), BF16),
            jax.ShapeDtypeStruct((m, LANE), F32),
        ],
        scratch_shapes=[pltpu.VMEM((tm, D_MODEL), BF16)],
        compiler_params=pltpu.CompilerParams(
            dimension_semantics=("arbitrary", "arbitrary"),
            vmem_limit_bytes=VMEM_LIMIT),
        name="proj",
    )(x2, w_main, w_lr)


def _gla_kernel(q_ref, k_ref, v_ref, go_ref, lr_ref, wup_ref, ba_ref, nw_ref,
                o_ref, st_ref, *, t_rows, n_parts):
    c = GLA_CHUNK
    cs = 2 * LANE

    @pl.when(pl.program_id(2) == 0)
    def _():
        st_ref[...] = jnp.zeros_like(st_ref)

    row = lax.broadcasted_iota(jnp.int32, (cs, cs), 0)
    col = lax.broadcasted_iota(jnp.int32, (cs, cs), 1)
    tri = jnp.where((row >= col) & (row // c == col // c), 1.0, 0.0).astype(BF16)
    r64 = lax.broadcasted_iota(jnp.int32, (c, c), 0)
    c64 = lax.broadcasted_iota(jnp.int32, (c, c), 1)
    causal = r64 >= c64

    half = t_rows // n_parts
    hc = half // c

    def stage_z(s):
        return jnp.dot(lr_ref[s:s + half, :].astype(BF16), wup_ref[...],
                       preferred_element_type=F32) + ba_ref[...]

    def stage_cumsum(z):
        softplus2 = jnp.log2(1.0 + jnp.exp2(jnp.abs(z) * (-LOG2E)))
        la = (jnp.minimum(z, 0.0) * LOG2E - softplus2) * (1.0 / GLA_TAU)
        la_hi = la.astype(BF16)
        la_lo = (la - la_hi.astype(F32)).astype(BF16)
        return jnp.concatenate(
            [jnp.dot(tri, la_hi[r:r + cs], preferred_element_type=F32)
             + jnp.dot(tri, la_lo[r:r + cs], preferred_element_type=F32)
             for r in range(0, half, cs)], axis=0)

    def stage_qk(s, b):
        b_last = [b[(i + 1) * c - 1:(i + 1) * c, :] for i in range(hc)]
        bl = jnp.concatenate([jnp.broadcast_to(r, (c, GLA_DK)) for r in b_last], axis=0)
        q = q_ref[s:s + half, :].astype(F32)
        k = k_ref[s:s + half, :].astype(F32)
        q_in = (q * jnp.exp2(b)).astype(BF16)
        k_in = (k * jnp.exp2(-b)).astype(BF16)
        k_dec = (k * jnp.exp2(bl - b)).astype(BF16)
        return q_in, k_in, k_dec, [jnp.exp2(r) for r in b_last]

    def stage_mm(s, q_in, k_in, k_dec):
        sub = [slice(i * c, (i + 1) * c) for i in range(hc)]
        scores = [
            jnp.where(causal,
                      lax.dot_general(q_in[r], k_in[r], (((1,), (1,)), ((), ())),
                                      preferred_element_type=F32),
                      0.0).astype(BF16)
            for r in sub]
        upds = [lax.dot_general(v_ref[s + r.start:s + r.stop, :], k_dec[r],
                                (((0,), (0,)), ((), ())),
                                preferred_element_type=F32)
                for r in sub]
        return scores, upds

    def stage_recur(s, st, q_in, scores, upds, decays, lo, hi):
        outs = []
        for i in range(lo, hi):
            r = slice(i * c, (i + 1) * c)
            outs.append(
                jnp.dot(scores[i], v_ref[s + r.start:s + r.stop, :],
                        preferred_element_type=F32)
                + lax.dot_general(q_in[r], st.astype(BF16), (((1,), (1,)), ((), ())),
                                  preferred_element_type=F32))
            st = st * decays[i] + upds[i]
        return st, outs

    def stage_out(s, outs):
        o = jnp.concatenate(outs, axis=0)
        rms = lax.rsqrt(jnp.mean(o * o, axis=-1, keepdims=True) + RMS_EPS * GLA_DK)
        g = go_ref[s:s + half, :].astype(F32)
        o_ref[s:s + half, :] = (o * rms * nw_ref[...]
                                * (g * jax.nn.sigmoid(g))).astype(BF16)

    starts = list(range(0, t_rows, half))
    st = st_ref[...]
    qk = stage_qk(starts[0], stage_cumsum(stage_z(starts[0])))
    for p, s in enumerate(starts):
        nxt = starts[p + 1] if p + 1 < len(starts) else None
        q_in, k_in, k_dec, decays = qk
        if nxt is not None:
            z_n = stage_z(nxt)
        scores, upds = stage_mm(s, q_in, k_in, k_dec)
        if nxt is not None:
            b_n = stage_cumsum(z_n)
        st, o1 = stage_recur(s, st, q_in, scores, upds, decays, 0, hc // 2)
        if nxt is not None:
            qk = stage_qk(nxt, b_n)
        st, o2 = stage_recur(s, st, q_in, scores, upds, decays, hc // 2, hc)
        stage_out(s, o1 + o2)
    st_ref[...] = st


def _gla(proj, glr, wup_pad, b_alpha2, norm_w2, batch, seq, t_rows=2048, n_parts=8):
    m = proj.shape[0]
    nt = seq // t_rows
    row = lambda b, h, t: b * nt + t
    return pl.pallas_call(
        functools.partial(_gla_kernel, t_rows=t_rows, n_parts=n_parts),
        grid=(batch, GLA_HEADS, nt),
        in_specs=[
            pl.BlockSpec((t_rows, GLA_DK), lambda b, h, t: (row(b, h, t), Q_OFF // GLA_DK + h)),
            pl.BlockSpec((t_rows, GLA_DK), lambda b, h, t: (row(b, h, t), K_OFF // GLA_DK + h)),
            pl.BlockSpec((t_rows, GLA_DV), lambda b, h, t: (row(b, h, t), V_OFF // GLA_DV + h)),
            pl.BlockSpec((t_rows, GLA_DV), lambda b, h, t: (row(b, h, t), GO_OFF // GLA_DV + h)),
            pl.BlockSpec((t_rows, LANE), lambda b, h, t: (row(b, h, t), 0)),
            pl.BlockSpec((LANE, GLA_DK), lambda b, h, t: (0, h)),
            pl.BlockSpec((1, GLA_DK), lambda b, h, t: (0, h)),
            pl.BlockSpec((1, GLA_DV), lambda b, h, t: (0, 0)),
        ],
        out_specs=pl.BlockSpec((t_rows, GLA_DV), lambda b, h, t: (row(b, h, t), h)),
        out_shape=jax.ShapeDtypeStruct((m, GLA_HEADS * GLA_DV), BF16),
        scratch_shapes=[pltpu.VMEM((GLA_DV, GLA_DK), F32)],
        compiler_params=pltpu.CompilerParams(
            dimension_semantics=("arbitrary", "arbitrary", "arbitrary"),
            vmem_limit_bytes=VMEM_LIMIT),
        name="gla",
    )(proj, proj, proj, proj, glr, wup_pad, b_alpha2, norm_w2)


def _alibi_slopes():
    h = np.arange(1, SWA_HEADS + 1, dtype=np.float32)
    return [float(s) for s in (2.0 ** (-8.0 * h / SWA_HEADS)).astype(np.float32)]


MASKED = -1e30
ONES_ROWS = 16
SWA_LOOKAHEAD = 5


def _swa_kernel(sink_ref, slope_ref, q_ref, kp_ref, kc_ref, vp_ref, vc_ref, o_ref,
                tbl_ref, *, q_blocks):
    w = WINDOW
    n = pl.program_id(1)

    @pl.when((pl.program_id(0) == 0) & (n == 0))
    def _():
        s = lax.broadcasted_iota(jnp.int32, (2 * w, w), 0)
        t = lax.broadcasted_iota(jnp.int32, (2 * w, w), 1)
        dist = t - s + w
        band = (dist >= 0) & (dist < w)
        ndist = -dist.astype(F32)

        def fill(h, carry):
            bias = (slope_ref[h] * LOG2E) * ndist
            tbl_ref[0, h] = jnp.where(band & (s >= w), bias, MASKED)
            tbl_ref[1, h] = jnp.where(band, bias, MASKED)
            return carry

        lax.fori_loop(0, SWA_HEADS, fill, 0)

    lane = lax.broadcasted_iota(jnp.int32, (2 * w, LANE), 1)
    low = lane < SWA_HD
    lane_q = lax.broadcasted_iota(jnp.int32, (w, LANE), 1)
    low_q = lane_q < SWA_HD
    ones = jnp.ones((ONES_ROWS, 2 * w), BF16)

    kall = jnp.concatenate([kp_ref[...], kc_ref[...]], axis=0)
    vall = jnp.concatenate([vp_ref[...], vc_ref[...]], axis=0)
    n_pairs = SWA_HEADS // 2
    pairs_per_kv = SWA_GROUP // 2

    def window(blk):
        kfull = kall[blk * w:(blk + 2) * w, :]
        v_t = vall[blk * w:(blk + 2) * w, :].astype(F32).T.astype(BF16)
        kdups = []
        for kvp in range(SWA_KV_HEADS // 2):
            kcol = kfull[:, kvp * LANE:(kvp + 1) * LANE].astype(F32) * LOG2E
            krol = pltpu.roll(kcol, SWA_HD, 1)
            kdups.append(jnp.where(low, kcol, krol).astype(BF16))
            kdups.append(jnp.where(low, krol, kcol).astype(BF16))
        vaugs = [jnp.concatenate([v_t[kv * SWA_HD:(kv + 1) * SWA_HD, :], ones], axis=0)
                 for kv in range(SWA_KV_HEADS)]
        return kdups, vaugs

    windows = [window(blk) for blk in range(q_blocks)]

    def logits_t(blk, p_idx):
        qp = (q_ref[blk * w:(blk + 1) * w, p_idx * LANE:(p_idx + 1) * LANE]
              * (SWA_HD ** -0.5))
        qw = jnp.concatenate([jnp.where(low_q, qp, 0.0),
                              jnp.where(low_q, 0.0, qp)], axis=0)
        return lax.dot_general(windows[blk][0][p_idx // pairs_per_kv], qw,
                               (((1,), (1,)), ((), ())),
                               preferred_element_type=F32)

    def finish(blk, p_idx, lg):
        var = jnp.minimum(n, 1) if blk == 0 else 1
        ps, ms = [], []
        for e in range(2):
            h = 2 * p_idx + e
            l = lg[:, e * w:(e + 1) * w] + tbl_ref[var, h]
            mx = jnp.maximum(jnp.max(l, axis=0, keepdims=True), sink_ref[h] * LOG2E)
            ps.append(jnp.exp2(l - mx).astype(BF16))
            ms.append(mx)
        ot = jnp.dot(windows[blk][1][p_idx // pairs_per_kv], jnp.concatenate(ps, axis=1),
                     preferred_element_type=F32)
        halves = []
        for e in range(2):
            h = 2 * p_idx + e
            den = (ot[SWA_HD:SWA_HD + 1, e * w:(e + 1) * w]
                   + jnp.exp2(sink_ref[h] * LOG2E - ms[e]))
            halves.append(ot[0:SWA_HD, e * w:(e + 1) * w] * (1.0 / den))
        o_ref[blk * w:(blk + 1) * w, p_idx * LANE:(p_idx + 1) * LANE] = (
            jnp.concatenate(halves, axis=0).T.astype(BF16))

    units = [(blk, p) for blk in range(q_blocks) for p in range(n_pairs)]
    pending = {u: logits_t(*u) for u in units[:SWA_LOOKAHEAD]}
    for idx, u in enumerate(units):
        finish(*u, pending.pop(u))
        if idx + SWA_LOOKAHEAD < len(units):
            nu = units[idx + SWA_LOOKAHEAD]
            pending[nu] = logits_t(*nu)


def _swa(proj, sinks, batch, seq, q_blocks=4):
    m = proj.shape[0]
    w = WINDOW
    tq = q_blocks * w
    ns = seq // tq
    n_heads_cols = SWA_HEADS * SWA_HD
    kvw = SWA_KV_HEADS * SWA_HD
    cur = lambda b, n: b * ns + n
    prev = lambda b, n: (b * ns + n) * q_blocks - jnp.minimum(n, 1)
    return pl.pallas_call(
        functools.partial(_swa_kernel, q_blocks=q_blocks),
        grid=(batch, ns),
        in_specs=[
            pl.BlockSpec(memory_space=pltpu.SMEM),
            pl.BlockSpec(memory_space=pltpu.SMEM),
            pl.BlockSpec((tq, n_heads_cols), lambda b, n: (cur(b, n), SQ_OFF // n_heads_cols)),
            pl.BlockSpec((w, kvw), lambda b, n: (prev(b, n), SK_OFF // kvw)),
            pl.BlockSpec((tq, kvw), lambda b, n: (cur(b, n), SK_OFF // kvw)),
            pl.BlockSpec((w, kvw), lambda b, n: (prev(b, n), SV_OFF // kvw)),
            pl.BlockSpec((tq, kvw), lambda b, n: (cur(b, n), SV_OFF // kvw)),
        ],
        out_specs=pl.BlockSpec((tq, n_heads_cols), lambda b, n: (cur(b, n), 0)),
        out_shape=jax.ShapeDtypeStruct((m, n_heads_cols), BF16),
        scratch_shapes=[pltpu.VMEM((2, SWA_HEADS, 2 * w, w), F32)],
        compiler_params=pltpu.CompilerParams(
            dimension_semantics=("arbitrary", "arbitrary"),
            vmem_limit_bytes=VMEM_LIMIT),
        name="swa",
    )(sinks, jnp.asarray(_alibi_slopes(), F32), proj, proj, proj, proj, proj)


def _merge_kernel(oa_ref, ob_ref, gg_ref, gs_ref, x_ref, wg_ref, ws_ref, wo_ref,
                  g_ref, b_ref, o_ref):
    ya = jnp.dot(oa_ref[...], wg_ref[...], preferred_element_type=F32)
    yb = jnp.dot(ob_ref[...], ws_ref[...], preferred_element_type=F32)
    merged = (jax.nn.sigmoid(gg_ref[...].astype(F32)) * ya
              + jax.nn.sigmoid(gs_ref[...].astype(F32)) * yb)
    mix = jnp.dot(merged.astype(BF16), wo_ref[...], preferred_element_type=F32)
    o_ref[...] = _layer_norm(ALPHA * x_ref[...] + mix, g_ref[...], b_ref[...])


def _merge(o_a, o_b, proj, x2, wg, ws, wo, ln_g, ln_b, tm=256):
    m = x2.shape[0]
    wspec = pl.BlockSpec((D_MODEL, D_MODEL), lambda i: (0, 0),
                         pipeline_mode=pl.Buffered(1))
    vspec = pl.BlockSpec((1, D_MODEL), lambda i: (0, 0))
    return pl.pallas_call(
        _merge_kernel,
        grid=(m // tm,),
        in_specs=[
            pl.BlockSpec((tm, D_MODEL), lambda i: (i, 0)),
            pl.BlockSpec((tm, D_MODEL), lambda i: (i, 0)),
            pl.BlockSpec((tm, D_MODEL), lambda i: (i, GG_OFF // D_MODEL)),
            pl.BlockSpec((tm, D_MODEL), lambda i: (i, GS_OFF // D_MODEL)),
            pl.BlockSpec((tm, D_MODEL), lambda i: (i, 0)),
            wspec, wspec, wspec, vspec, vspec,
        ],
        out_specs=pl.BlockSpec((tm, D_MODEL), lambda i: (i, 0)),
        out_shape=jax.ShapeDtypeStruct((m, D_MODEL), F32),
        compiler_params=pltpu.CompilerParams(
            dimension_semantics=("arbitrary",),
            vmem_limit_bytes=VMEM_LIMIT),
        name="merge",
    )(o_a, o_b, proj, proj, x2, wg, ws, wo, ln_g, ln_b)


def _ffn_kernel(x_ref, wu_ref, wd_ref, g_ref, b_ref, o_ref, xb_ref):
    j = pl.program_id(1)

    @pl.when(j == 0)
    def _():
        xb_ref[...] = x_ref[...].astype(BF16)
        o_ref[...] = jnp.zeros_like(o_ref)

    h = jnp.dot(xb_ref[...], wu_ref[...], preferred_element_type=F32)
    h = jnp.maximum(h, 0.0)
    h = (h * h).astype(BF16)
    o_ref[...] += jnp.dot(h, wd_ref[...], preferred_element_type=F32)

    @pl.when(j == pl.num_programs(1) - 1)
    def _():
        o_ref[...] = _layer_norm(ALPHA * x_ref[...] + o_ref[...],
                                 g_ref[...], b_ref[...])


def _ffn(x1, wu, wd, ln_g, ln_b, tm=512, tf=1024):
    m = x1.shape[0]
    vspec = pl.BlockSpec((1, D_MODEL), lambda i, j: (0, 0))
    return pl.pallas_call(
        _ffn_kernel,
        grid=(m // tm, D_FF // tf),
        in_specs=[
            pl.BlockSpec((tm, D_MODEL), lambda i, j: (i, 0)),
            pl.BlockSpec((D_MODEL, tf), lambda i, j: (0, j)),
            pl.BlockSpec((tf, D_MODEL), lambda i, j: (j, 0)),
            vspec, vspec,
        ],
        out_specs=pl.BlockSpec((tm, D_MODEL), lambda i, j: (i, 0)),
        out_shape=jax.ShapeDtypeStruct((m, D_MODEL), F32),
        scratch_shapes=[pltpu.VMEM((tm, D_MODEL), BF16)],
        compiler_params=pltpu.CompilerParams(
            dimension_semantics=("arbitrary", "arbitrary"),
            vmem_limit_bytes=VMEM_LIMIT),
        name="ffn",
    )(x1, wu, wd, ln_g, ln_b)


def kernel(x, w_in, w_alpha_up, b_alpha, gla_norm_w, attn_sinks, w_branch_gla,
           w_branch_swa, w_out, ln1_g, ln1_b, w_ff_up, w_ff_down, ln2_g, ln2_b):
    batch, seq, d = x.shape
    x2 = x.reshape(batch * seq, d)

    w_main = jnp.concatenate(
        [w_in[:, 0:6144], w_in[:, 6160:8208], w_in[:, 8720:10768],
         w_in[:, 10768:12816], w_in[:, 8208:8464], w_in[:, 8464:8720]],
        axis=1).astype(BF16)
    w_lr = jnp.pad(w_in[:, 6144:6160], ((0, 0), (0, LANE - GLA_RANK))).astype(BF16)
    wup_pad = jnp.pad(w_alpha_up, ((0, LANE - GLA_RANK), (0, 0))).astype(BF16)

    proj, glr = _proj(x2, w_main, w_lr)
    o_a = _gla(proj, glr, wup_pad, b_alpha.reshape(1, -1).astype(F32),
               gla_norm_w.reshape(1, -1).astype(F32), batch, seq)
    o_b = _swa(proj, attn_sinks.astype(F32), batch, seq)
    x1 = _merge(o_a, o_b, proj, x2, w_branch_gla.astype(BF16),
                w_branch_swa.astype(BF16), w_out.astype(BF16),
                ln1_g.reshape(1, -1), ln1_b.reshape(1, -1))
    out = _ffn(x1, w_ff_up.astype(BF16), w_ff_down.astype(BF16),
               ln2_g.reshape(1, -1), ln2_b.reshape(1, -1))
    return out.reshape(batch, seq, d)
```

```python
import functools

import jax
import jax.numpy as jnp
import numpy as np
from jax import lax
from jax.experimental import pallas as pl
from jax.experimental.pallas import tpu as pltpu

F32 = jnp.float32
BF16 = jnp.bfloat16

D_MODEL = 2048
GLA_HEADS = 4
GLA_DK = 256
GLA_DV = 512
GLA_RANK = 16
GLA_TAU = 16.0
GLA_CHUNK = 64
SWA_HEADS = 32
SWA_KV_HEADS = 4
SWA_HD = 64
SWA_GROUP = 8
WINDOW = 128
D_FF = 4 * D_MODEL
ALPHA = 2.0 ** 0.25
LN_EPS = 1e-5
RMS_EPS = 1e-6
LOG2E = 1.4426950408889634

Q_OFF, K_OFF, V_OFF, GO_OFF, SQ_OFF, GG_OFF, GS_OFF, SK_OFF, SV_OFF = (
    0, 1024, 2048, 4096, 6144, 8192, 10240, 12288, 12544)
N_PROJ = 12800
LANE = 128

VMEM_LIMIT = 56 * 1024 * 1024


def _layer_norm(z, g, b):
    mu = jnp.mean(z, axis=-1, keepdims=True)
    zc = z - mu
    var = jnp.mean(zc * zc, axis=-1, keepdims=True)
    return zc * lax.rsqrt(var + LN_EPS) * g + b


def _zero_after(y):
    bits = lax.bitcast_convert_type(y, jnp.uint32)
    acc = bits[:, 0:LANE]
    for c0 in range(LANE, y.shape[1], LANE):
        acc = acc | bits[:, c0:c0 + LANE]
    return ((acc >> 16) >> 16).astype(F32)


def _proj_kernel(x_ref, w_ref, wlr_ref, o_ref, lr_ref, xb_ref):
    j = pl.program_id(1)

    @pl.when(j == 0)
    def _():
        xb = x_ref[...].astype(BF16)
        xb_ref[...] = xb
        o_ref[...] = jnp.dot(xb, w_ref[...], preferred_element_type=F32).astype(BF16)
        lr_ref[...] = jnp.dot(xb, wlr_ref[...], preferred_element_type=F32)

    @pl.when(j > 0)
    def _():
        o_ref[...] = jnp.dot(xb_ref[...], w_ref[...],
                             preferred_element_type=F32).astype(BF16)


def _proj(x2, w_main, w_lr, tm=1024, tn=2560):
    m = x2.shape[0]
    return pl.pallas_call(
        _proj_kernel,
        grid=(m // tm, N_PROJ // tn),
        in_specs=[
            pl.BlockSpec((tm, D_MODEL), lambda i, j: (i, 0)),
            pl.BlockSpec((D_MODEL, tn), lambda i, j: (0, j)),
            pl.BlockSpec((D_MODEL, LANE), lambda i, j: (0, 0)),
        ],
        out_specs=[
            pl.BlockSpec((tm, tn), lambda i, j: (i, j)),
            pl.BlockSpec((tm, LANE), lambda i, j: (i, 0)),
        ],
        out_shape=[
            jax.ShapeDtypeStruct((m, N_PROJ), BF16),
            jax.ShapeDtypeStruct((m, LANE), F32),
        ],
        scratch_shapes=[pltpu.VMEM((tm, D_MODEL), BF16)],
        compiler_params=pltpu.CompilerParams(
            dimension_semantics=("arbitrary", "arbitrary"),
            vmem_limit_bytes=VMEM_LIMIT),
        name="proj",
    )(x2, w_main, w_lr)


def _gla_kernel(q_ref, k_ref, v_ref, go_ref, lr_ref, wup_ref, ba_ref, nw_ref,
                o_ref, st_ref, *, t_rows, n_parts):
    c = GLA_CHUNK
    cs = 2 * LANE

    @pl.when(pl.program_id(2) == 0)
    def _():
        st_ref[...] = jnp.zeros_like(st_ref)

    row = lax.broadcasted_iota(jnp.int32, (cs, cs), 0)
    col = lax.broadcasted_iota(jnp.int32, (cs, cs), 1)
    tri = jnp.where((row >= col) & (row // c == col // c), 1.0, 0.0).astype(BF16)
    r64 = lax.broadcasted_iota(jnp.int32, (c, c), 0)
    c64 = lax.broadcasted_iota(jnp.int32, (c, c), 1)
    causal = r64 >= c64

    half = t_rows // n_parts
    hc = half // c

    def stage_z(s):
        return jnp.dot(lr_ref[s:s + half, :].astype(BF16), wup_ref[...],
                       preferred_element_type=F32) + ba_ref[...]

    def stage_cumsum(z):
        softplus2 = jnp.log2(1.0 + jnp.exp2(jnp.abs(z) * (-LOG2E)))
        la = (jnp.minimum(z, 0.0) * LOG2E - softplus2) * (1.0 / GLA_TAU)
        la_hi = la.astype(BF16)
        la_lo = (la - la_hi.astype(F32)).astype(BF16)
        return jnp.concatenate(
            [jnp.dot(tri, la_hi[r:r + cs], preferred_element_type=F32)
             + jnp.dot(tri, la_lo[r:r + cs], preferred_element_type=F32)
             for r in range(0, half, cs)], axis=0)

    def stage_qk(s, b):
        b_last = [b[(i + 1) * c - 1:(i + 1) * c, :] for i in range(hc)]
        bl = jnp.concatenate([jnp.broadcast_to(r, (c, GLA_DK)) for r in b_last], axis=0)
        q = q_ref[s:s + half, :].astype(F32)
        k = k_ref[s:s + half, :].astype(F32)
        q_in = (q * jnp.exp2(b)).astype(BF16)
        k_in = (k * jnp.exp2(-b)).astype(BF16)
        k_dec = (k * jnp.exp2(bl - b)).astype(BF16)
        return q_in, k_in, k_dec, [jnp.exp2(r) for r in b_last]

    def stage_mm(s, q_in, k_in, k_dec):
        sub = [slice(i * c, (i + 1) * c) for i in range(hc)]
        scores = [
            jnp.where(causal,
                      lax.dot_general(q_in[r], k_in[r], (((1,), (1,)), ((), ())),
                                      preferred_element_type=F32),
                      0.0).astype(BF16)
            for r in sub]
        upds = [lax.dot_general(v_ref[s + r.start:s + r.stop, :], k_dec[r],
                                (((0,), (0,)), ((), ())),
                                preferred_element_type=F32)
                for r in sub]
        return scores, upds

    def stage_recur(s, st, q_in, scores, upds, decays, lo, hi):
        outs = []
        for i in range(lo, hi):
            r = slice(i * c, (i + 1) * c)
            outs.append(
                jnp.dot(scores[i], v_ref[s + r.start:s + r.stop, :],
                        preferred_element_type=F32)
                + lax.dot_general(q_in[r], st.astype(BF16), (((1,), (1,)), ((), ())),
                                  preferred_element_type=F32))
            st = st * decays[i] + upds[i]
        return st, outs

    def stage_out(s, outs):
        o = jnp.concatenate(outs, axis=0)
        rms = lax.rsqrt(jnp.mean(o * o, axis=-1, keepdims=True) + RMS_EPS * GLA_DK)
        g = go_ref[s:s + half, :].astype(F32)
        o_ref[s:s + half, :] = (o * rms * nw_ref[...]
                                * (g * jax.nn.sigmoid(g))).astype(BF16)

    starts = list(range(0, t_rows, half))
    st = st_ref[...]
    qk = stage_qk(starts[0], stage_cumsum(stage_z(starts[0])))
    for p, s in enumerate(starts):
        nxt = starts[p + 1] if p + 1 < len(starts) else None
        q_in, k_in, k_dec, decays = qk
        if nxt is not None:
            z_n = stage_z(nxt)
        scores, upds = stage_mm(s, q_in, k_in, k_dec)
        if nxt is not None:
            b_n = stage_cumsum(z_n)
        st, o1 = stage_recur(s, st, q_in, scores, upds, decays, 0, hc // 2)
        if nxt is not None:
            qk = stage_qk(nxt, b_n)
        st, o2 = stage_recur(s, st, q_in, scores, upds, decays, hc // 2, hc)
        stage_out(s, o1 + o2)
    st_ref[...] = st


def _gla(proj, glr, wup_pad, b_alpha2, norm_w2, batch, seq, t_rows=2048, n_parts=8):
    m = proj.shape[0]
    nt = seq // t_rows
    row = lambda b, h, t: b * nt + t
    return pl.pallas_call(
        functools.partial(_gla_kernel, t_rows=t_rows, n_parts=n_parts),
        grid=(batch, GLA_HEADS, nt),
        in_specs=[
            pl.BlockSpec((t_rows, GLA_DK), lambda b, h, t: (row(b, h, t), Q_OFF // GLA_DK + h)),
            pl.BlockSpec((t_rows, GLA_DK), lambda b, h, t: (row(b, h, t), K_OFF // GLA_DK + h)),
            pl.BlockSpec((t_rows, GLA_DV), lambda b, h, t: (row(b, h, t), V_OFF // GLA_DV + h)),
            pl.BlockSpec((t_rows, GLA_DV), lambda b, h, t: (row(b, h, t), GO_OFF // GLA_DV + h)),
            pl.BlockSpec((t_rows, LANE), lambda b, h, t: (row(b, h, t), 0)),
            pl.BlockSpec((LANE, GLA_DK), lambda b, h, t: (0, h)),
            pl.BlockSpec((1, GLA_DK), lambda b, h, t: (0, h)),
            pl.BlockSpec((1, GLA_DV), lambda b, h, t: (0, 0)),
        ],
        out_specs=pl.BlockSpec((t_rows, GLA_DV), lambda b, h, t: (row(b, h, t), h)),
        out_shape=jax.ShapeDtypeStruct((m, GLA_HEADS * GLA_DV), BF16),
        scratch_shapes=[pltpu.VMEM((GLA_DV, GLA_DK), F32)],
        compiler_params=pltpu.CompilerParams(
            dimension_semantics=("arbitrary", "arbitrary", "arbitrary"),
            vmem_limit_bytes=VMEM_LIMIT),
        name="gla",
    )(proj, proj, proj, proj, glr, wup_pad, b_alpha2, norm_w2)


def _alibi_slopes():
    h = np.arange(1, SWA_HEADS + 1, dtype=np.float32)
    return [float(s) for s in (2.0 ** (-8.0 * h / SWA_HEADS)).astype(np.float32)]


MASKED = -1e30
ONES_ROWS = 16
SWA_LOOKAHEAD = 5


def _swa_kernel(sink_ref, slope_ref, q_ref, kp_ref, kc_ref, vp_ref, vc_ref, o_ref,
                tbl_ref, *, q_blocks):
    w = WINDOW
    n = pl.program_id(1)

    @pl.when((pl.program_id(0) == 0) & (n == 0))
    def _():
        s = lax.broadcasted_iota(jnp.int32, (2 * w, w), 0)
        t = lax.broadcasted_iota(jnp.int32, (2 * w, w), 1)
        dist = t - s + w
        band = (dist >= 0) & (dist < w)
        ndist = -dist.astype(F32)

        def fill(h, carry):
            bias = (slope_ref[h] * LOG2E) * ndist
            tbl_ref[0, h] = jnp.where(band & (s >= w), bias, MASKED)
            tbl_ref[1, h] = jnp.where(band, bias, MASKED)
            return carry

        lax.fori_loop(0, SWA_HEADS, fill, 0)

    lane = lax.broadcasted_iota(jnp.int32, (2 * w, LANE), 1)
    low = lane < SWA_HD
    lane_q = lax.broadcasted_iota(jnp.int32, (w, LANE), 1)
    low_q = lane_q < SWA_HD
    ones = jnp.ones((ONES_ROWS, 2 * w), BF16)

    kall = jnp.concatenate([kp_ref[...], kc_ref[...]], axis=0)
    vall = jnp.concatenate([vp_ref[...], vc_ref[...]], axis=0)
    n_pairs = SWA_HEADS // 2
    pairs_per_kv = SWA_GROUP // 2

    def window(blk):
        kfull = kall[blk * w:(blk + 2) * w, :]
        v_t = vall[blk * w:(blk + 2) * w, :].astype(F32).T.astype(BF16)
        kdups = []
        for kvp in range(SWA_KV_HEADS // 2):
            kcol = kfull[:, kvp * LANE:(kvp + 1) * LANE].astype(F32) * LOG2E
            krol = pltpu.roll(kcol, SWA_HD, 1)
            kdups.append(jnp.where(low, kcol, krol).astype(BF16))
            kdups.append(jnp.where(low, krol, kcol).astype(BF16))
        vaugs = [jnp.concatenate([v_t[kv * SWA_HD:(kv + 1) * SWA_HD, :], ones], axis=0)
                 for kv in range(SWA_KV_HEADS)]
        return kdups, vaugs

    windows = [window(blk) for blk in range(q_blocks)]

    def logits_t(blk, p_idx):
        qp = (q_ref[blk * w:(blk + 1) * w, p_idx * LANE:(p_idx + 1) * LANE]
              * (SWA_HD ** -0.5))
        qw = jnp.concatenate([jnp.where(low_q, qp, 0.0),
                              jnp.where(low_q, 0.0, qp)], axis=0)
        return lax.dot_general(windows[blk][0][p_idx // pairs_per_kv], qw,
                               (((1,), (1,)), ((), ())),
                               preferred_element_type=F32)

    def finish(blk, p_idx, lg):
        var = jnp.minimum(n, 1) if blk == 0 else 1
        ps, ms = [], []
        for e in range(2):
            h = 2 * p_idx + e
            l = lg[:, e * w:(e + 1) * w] + tbl_ref[var, h]
            mx = jnp.maximum(jnp.max(l, axis=0, keepdims=True), sink_ref[h] * LOG2E)
            ps.append(jnp.exp2(l - mx).astype(BF16))
            ms.append(mx)
        ot = jnp.dot(windows[blk][1][p_idx // pairs_per_kv], jnp.concatenate(ps, axis=1),
                     preferred_element_type=F32)
        halves = []
        for e in range(2):
            h = 2 * p_idx + e
            den = (ot[SWA_HD:SWA_HD + 1, e * w:(e + 1) * w]
                   + jnp.exp2(sink_ref[h] * LOG2E - ms[e]))
            halves.append(ot[0:SWA_HD, e * w:(e + 1) * w] * (1.0 / den))
        o_ref[blk * w:(blk + 1) * w, p_idx * LANE:(p_idx + 1) * LANE] = (
            jnp.concatenate(halves, axis=0).T.astype(BF16))

    units = [(blk, p) for blk in range(q_blocks) for p in range(n_pairs)]
    pending = {u: logits_t(*u) for u in units[:SWA_LOOKAHEAD]}
    for idx, u in enumerate(units):
        finish(*u, pending.pop(u))
        if idx + SWA_LOOKAHEAD < len(units):
            nu = units[idx + SWA_LOOKAHEAD]
            pending[nu] = logits_t(*nu)


def _swa(proj, sinks, batch, seq, q_blocks=4):
    m = proj.shape[0]
    w = WINDOW
    tq = q_blocks * w
    ns = seq // tq
    n_heads_cols = SWA_HEADS * SWA_HD
    kvw = SWA_KV_HEADS * SWA_HD
    cur = lambda b, n: b * ns + n
    prev = lambda b, n: (b * ns + n) * q_blocks - jnp.minimum(n, 1)
    return pl.pallas_call(
        functools.partial(_swa_kernel, q_blocks=q_blocks),
        grid=(batch, ns),
        in_specs=[
            pl.BlockSpec(memory_space=pltpu.SMEM),
            pl.BlockSpec(memory_space=pltpu.SMEM),
            pl.BlockSpec((tq, n_heads_cols), lambda b, n: (cur(b, n), SQ_OFF // n_heads_cols)),
            pl.BlockSpec((w, kvw), lambda b, n: (prev(b, n), SK_OFF // kvw)),
            pl.BlockSpec((tq, kvw), lambda b, n: (cur(b, n), SK_OFF // kvw)),
            pl.BlockSpec((w, kvw), lambda b, n: (prev(b, n), SV_OFF // kvw)),
            pl.BlockSpec((tq, kvw), lambda b, n: (cur(b, n), SV_OFF // kvw)),
        ],
        out_specs=pl.BlockSpec((tq, n_heads_cols), lambda b, n: (cur(b, n), 0)),
        out_shape=jax.ShapeDtypeStruct((m, n_heads_cols), BF16),
        scratch_shapes=[pltpu.VMEM((2, SWA_HEADS, 2 * w, w), F32)],
        compiler_params=pltpu.CompilerParams(
            dimension_semantics=("arbitrary", "arbitrary"),
            vmem_limit_bytes=VMEM_LIMIT),
        name="swa",
    )(sinks, jnp.asarray(_alibi_slopes(), F32), proj, proj, proj, proj, proj)


def _merge_kernel(oa_ref, ob_ref, gg_ref, gs_ref, x_ref, wg_ref, ws_ref, wo_ref,
                  g_ref, b_ref, o_ref, z_ref):
    i = pl.program_id(0)
    slot = i % 2

    @pl.when(i == 0)
    def _():
        z_ref[1] = jnp.zeros(z_ref.shape[1:], F32)

    ln = _layer_norm(z_ref[1 - slot], g_ref[...], b_ref[...])
    o_ref[...] = ln
    ya = jnp.dot(oa_ref[...], wg_ref[...], preferred_element_type=F32)
    yb = jnp.dot(ob_ref[...], ws_ref[...], preferred_element_type=F32)
    merged = (jax.nn.sigmoid(gg_ref[...].astype(F32)) * ya
              + jax.nn.sigmoid(gs_ref[...].astype(F32)) * yb)
    merged = jnp.concatenate([merged[:, :LANE] + _zero_after(ln), merged[:, LANE:]], axis=1)
    mix = jnp.dot(merged.astype(BF16), wo_ref[...], preferred_element_type=F32)
    z_ref[slot] = ALPHA * x_ref[...] + mix


def _merge(o_a, o_b, proj, x2, wg, ws, wo, ln_g, ln_b, tm=256):
    m = x2.shape[0]
    nt = m // tm
    wspec = pl.BlockSpec((D_MODEL, D_MODEL), lambda i: (0, 0),
                         pipeline_mode=pl.Buffered(1))
    vspec = pl.BlockSpec((1, D_MODEL), lambda i: (0, 0))
    src = lambda i: jnp.minimum(i, nt - 1)
    return pl.pallas_call(
        _merge_kernel,
        grid=(nt + 1,),
        in_specs=[
            pl.BlockSpec((tm, D_MODEL), lambda i: (src(i), 0)),
            pl.BlockSpec((tm, D_MODEL), lambda i: (src(i), 0)),
            pl.BlockSpec((tm, D_MODEL), lambda i: (src(i), GG_OFF // D_MODEL)),
            pl.BlockSpec((tm, D_MODEL), lambda i: (src(i), GS_OFF // D_MODEL)),
            pl.BlockSpec((tm, D_MODEL), lambda i: (src(i), 0)),
            wspec, wspec, wspec, vspec, vspec,
        ],
        out_specs=pl.BlockSpec((tm, D_MODEL), lambda i: (jnp.maximum(i - 1, 0), 0)),
        out_shape=jax.ShapeDtypeStruct((m, D_MODEL), F32),
        scratch_shapes=[pltpu.VMEM((2, tm, D_MODEL), F32)],
        compiler_params=pltpu.CompilerParams(
            dimension_semantics=("arbitrary",),
            vmem_limit_bytes=VMEM_LIMIT),
        name="merge",
    )(o_a, o_b, proj, proj, x2, wg, ws, wo, ln_g, ln_b)


def _ffn_kernel(x_ref, wu_ref, wd_ref, g_ref, b_ref, o_ref, xb_ref, acc_ref, *,
                n_tiles):
    i = pl.program_id(0)
    j = pl.program_id(1)
    slot = i % 2

    def mlp_part(xb, anchor=None):
        h = jnp.dot(xb, wu_ref[...], preferred_element_type=F32)
        h = jnp.maximum(h, 0.0)
        h = h * h
        if anchor is not None:
            h = jnp.concatenate([h[:, :LANE] + anchor, h[:, LANE:]], axis=1)
        return jnp.dot(h.astype(BF16), wd_ref[...], preferred_element_type=F32)

    @pl.when((i == 0) & (j == 0))
    def _():
        acc_ref[1] = jnp.zeros(acc_ref.shape[1:], F32)

    @pl.when(j == 0)
    def _():
        ln = _layer_norm(acc_ref[1 - slot], g_ref[...], b_ref[...])
        o_ref[...] = ln
        x = x_ref[...]
        xb = x.astype(BF16)
        xb_ref[...] = xb
        acc_ref[slot] = ALPHA * x + mlp_part(xb, _zero_after(ln))

    @pl.when((j > 0) & (i < n_tiles))
    def _():
        acc_ref[slot] += mlp_part(xb_ref[...])


def _ffn(x1, wu, wd, ln_g, ln_b, tm=512, tf=1024):
    m = x1.shape[0]
    nt = m // tm
    vspec = pl.BlockSpec((1, D_MODEL), lambda i, j: (0, 0))
    return pl.pallas_call(
        functools.partial(_ffn_kernel, n_tiles=nt),
        grid=(nt + 1, D_FF // tf),
        in_specs=[
            pl.BlockSpec((tm, D_MODEL), lambda i, j: (jnp.minimum(i, nt - 1), 0)),
            pl.BlockSpec((D_MODEL, tf), lambda i, j: (0, jnp.where(i < nt, j, 0))),
            pl.BlockSpec((tf, D_MODEL), lambda i, j: (jnp.where(i < nt, j, 0), 0)),
            vspec, vspec,
        ],
        out_specs=pl.BlockSpec((tm, D_MODEL), lambda i, j: (jnp.maximum(i - 1, 0), 0)),
        out_shape=jax.ShapeDtypeStruct((m, D_MODEL), F32),
        scratch_shapes=[pltpu.VMEM((tm, D_MODEL), BF16),
                        pltpu.VMEM((2, tm, D_MODEL), F32)],
        compiler_params=pltpu.CompilerParams(
            dimension_semantics=("arbitrary", "arbitrary"),
            vmem_limit_bytes=VMEM_LIMIT),
        name="ffn",
    )(x1, wu, wd, ln_g, ln_b)


def kernel(x, w_in, w_alpha_up, b_alpha, gla_norm_w, attn_sinks, w_branch_gla,
           w_branch_swa, w_out, ln1_g, ln1_b, w_ff_up, w_ff_down, ln2_g, ln2_b):
    batch, seq, d = x.shape
    x2 = x.reshape(batch * seq, d)

    w_main = jnp.concatenate(
        [w_in[:, 0:6144], w_in[:, 6160:8208], w_in[:, 8720:10768],
         w_in[:, 10768:12816], w_in[:, 8208:8464], w_in[:, 8464:8720]],
        axis=1).astype(BF16)
    w_lr = jnp.pad(w_in[:, 6144:6160], ((0, 0), (0, LANE - GLA_RANK))).astype(BF16)
    wup_pad = jnp.pad(w_alpha_up, ((0, LANE - GLA_RANK), (0, 0))).astype(BF16)

    proj, glr = _proj(x2, w_main, w_lr)
    o_a = _gla(proj, glr, wup_pad, b_alpha.reshape(1, -1).astype(F32),
               gla_norm_w.reshape(1, -1).astype(F32), batch, seq)
    o_b = _swa(proj, attn_sinks.astype(F32), batch, seq)
    x1 = _merge(o_a, o_b, proj, x2, w_branch_gla.astype(BF16),
                w_branch_swa.astype(BF16), w_out.astype(BF16),
                ln1_g.reshape(1, -1), ln1_b.reshape(1, -1))
    out = _ffn(x1, w_ff_up.astype(BF16), w_ff_down.astype(BF16),
               ln2_g.reshape(1, -1), ln2_b.reshape(1, -1))
    return out.reshape(batch, seq, d)
```

```python
import functools

import jax
import jax.numpy as jnp
import numpy as np
from jax import lax
from jax.experimental import pallas as pl
from jax.experimental.pallas import tpu as pltpu

F32 = jnp.float32
BF16 = jnp.bfloat16

D_MODEL = 2048
GLA_HEADS = 4
GLA_DK = 256
GLA_DV = 512
GLA_RANK = 16
GLA_TAU = 16.0
GLA_CHUNK = 64
SWA_HEADS = 32
SWA_KV_HEADS = 4
SWA_HD = 64
SWA_GROUP = 8
WINDOW = 128
D_FF = 4 * D_MODEL
ALPHA = 2.0 ** 0.25
LN_EPS = 1e-5
RMS_EPS = 1e-6
LOG2E = 1.4426950408889634

Q_OFF, K_OFF, V_OFF, GO_OFF, SQ_OFF, GG_OFF, GS_OFF, SK_OFF, SV_OFF = (
    0, 1024, 2048, 4096, 6144, 8192, 10240, 12288, 12544)
N_PROJ = 12800
LANE = 128

VMEM_LIMIT = 56 * 1024 * 1024
FFN_VMEM_LIMIT = 60 * 1024 * 1024


def _layer_norm(z, g, b):
    mu = jnp.mean(z, axis=-1, keepdims=True)
    zc = z - mu
    var = jnp.mean(zc * zc, axis=-1, keepdims=True)
    return zc * lax.rsqrt(var + LN_EPS) * g + b


def _zero_after(y):
    bits = lax.bitcast_convert_type(y, jnp.uint32)
    acc = bits[:, 0:LANE]
    for c0 in range(LANE, y.shape[1], LANE):
        acc = acc | bits[:, c0:c0 + LANE]
    return ((acc >> 16) >> 16).astype(F32)


def _proj_kernel(x_ref, w_ref, wlr_ref, o_ref, lr_ref, xb_ref):
    j = pl.program_id(1)

    @pl.when(j == 0)
    def _():
        xb = x_ref[...].astype(BF16)
        xb_ref[...] = xb
        o_ref[...] = jnp.dot(xb, w_ref[...], preferred_element_type=F32).astype(BF16)
        lr_ref[...] = jnp.dot(xb, wlr_ref[...], preferred_element_type=F32)

    @pl.when(j > 0)
    def _():
        o_ref[...] = jnp.dot(xb_ref[...], w_ref[...],
                             preferred_element_type=F32).astype(BF16)


def _proj(x2, w_main, w_lr, tm=1024, tn=2560):
    m = x2.shape[0]
    return pl.pallas_call(
        _proj_kernel,
        grid=(m // tm, N_PROJ // tn),
        in_specs=[
            pl.BlockSpec((tm, D_MODEL), lambda i, j: (i, 0)),
            pl.BlockSpec((D_MODEL, tn), lambda i, j: (0, j)),
            pl.BlockSpec((D_MODEL, LANE), lambda i, j: (0, 0)),
        ],
        out_specs=[
            pl.BlockSpec((tm, tn), lambda i, j: (i, j)),
            pl.BlockSpec((tm, LANE), lambda i, j: (i, 0)),
        ],
        out_shape=[
            jax.ShapeDtypeStruct((m, N_PROJ), BF16),
            jax.ShapeDtypeStruct((m, LANE), F32),
        ],
        scratch_shapes=[pltpu.VMEM((tm, D_MODEL), BF16)],
        compiler_params=pltpu.CompilerParams(
            dimension_semantics=("arbitrary", "arbitrary"),
            vmem_limit_bytes=VMEM_LIMIT),
        name="proj",
    )(x2, w_main, w_lr)


def _gla_kernel(q_ref, k_ref, v_ref, go_ref, lr_ref, wup_ref, ba_ref, nw_ref,
                o_ref, st_ref, *, t_rows, n_parts):
    c = GLA_CHUNK
    cs = 2 * LANE

    @pl.when(pl.program_id(2) == 0)
    def _():
        st_ref[...] = jnp.zeros_like(st_ref)

    row = lax.broadcasted_iota(jnp.int32, (cs, cs), 0)
    col = lax.broadcasted_iota(jnp.int32, (cs, cs), 1)
    tri = jnp.where((row >= col) & (row // c == col // c), 1.0, 0.0).astype(BF16)
    r64 = lax.broadcasted_iota(jnp.int32, (c, c), 0)
    c64 = lax.broadcasted_iota(jnp.int32, (c, c), 1)
    causal = r64 >= c64

    half = t_rows // n_parts
    hc = half // c

    def stage_z(s):
        return jnp.dot(lr_ref[s:s + half, :].astype(BF16), wup_ref[...],
                       preferred_element_type=F32) + ba_ref[...]

    def stage_cumsum(z):
        softplus2 = jnp.log2(1.0 + jnp.exp2(jnp.abs(z) * (-LOG2E)))
        la = (jnp.minimum(z, 0.0) * LOG2E - softplus2) * (1.0 / GLA_TAU)
        la_hi = la.astype(BF16)
        la_lo = (la - la_hi.astype(F32)).astype(BF16)
        return jnp.concatenate(
            [jnp.dot(tri, la_hi[r:r + cs], preferred_element_type=F32)
             + jnp.dot(tri, la_lo[r:r + cs], preferred_element_type=F32)
             for r in range(0, half, cs)], axis=0)

    def stage_qk(s, b):
        b_last = [b[(i + 1) * c - 1:(i + 1) * c, :] for i in range(hc)]
        bl = jnp.concatenate([jnp.broadcast_to(r, (c, GLA_DK)) for r in b_last], axis=0)
        q = q_ref[s:s + half, :].astype(F32)
        k = k_ref[s:s + half, :].astype(F32)
        q_in = (q * jnp.exp2(b)).astype(BF16)
        k_in = (k * jnp.exp2(-b)).astype(BF16)
        k_dec = (k * jnp.exp2(bl - b)).astype(BF16)
        return q_in, k_in, k_dec, [jnp.exp2(r) for r in b_last]

    def stage_mm(s, q_in, k_in, k_dec):
        sub = [slice(i * c, (i + 1) * c) for i in range(hc)]
        scores = [
            jnp.where(causal,
                      lax.dot_general(q_in[r], k_in[r], (((1,), (1,)), ((), ())),
                                      preferred_element_type=F32),
                      0.0).astype(BF16)
            for r in sub]
        upds = [lax.dot_general(v_ref[s + r.start:s + r.stop, :], k_dec[r],
                                (((0,), (0,)), ((), ())),
                                preferred_element_type=F32)
                for r in sub]
        return scores, upds

    def stage_recur(s, st, q_in, scores, upds, decays, lo, hi):
        outs = []
        for i in range(lo, hi):
            r = slice(i * c, (i + 1) * c)
            outs.append(
                jnp.dot(scores[i], v_ref[s + r.start:s + r.stop, :],
                        preferred_element_type=F32)
                + lax.dot_general(q_in[r], st.astype(BF16), (((1,), (1,)), ((), ())),
                                  preferred_element_type=F32))
            st = st * decays[i] + upds[i]
        return st, outs

    def stage_out(s, outs):
        o = jnp.concatenate(outs, axis=0)
        rms = lax.rsqrt(jnp.mean(o * o, axis=-1, keepdims=True) + RMS_EPS * GLA_DK)
        g = go_ref[s:s + half, :].astype(F32)
        o_ref[s:s + half, :] = (o * rms * nw_ref[...]
                                * (g * jax.nn.sigmoid(g))).astype(BF16)

    starts = list(range(0, t_rows, half))
    st = st_ref[...]
    qk = stage_qk(starts[0], stage_cumsum(stage_z(starts[0])))
    for p, s in enumerate(starts):
        nxt = starts[p + 1] if p + 1 < len(starts) else None
        q_in, k_in, k_dec, decays = qk
        if nxt is not None:
            z_n = stage_z(nxt)
        scores, upds = stage_mm(s, q_in, k_in, k_dec)
        if nxt is not None:
            b_n = stage_cumsum(z_n)
        st, o1 = stage_recur(s, st, q_in, scores, upds, decays, 0, hc // 2)
        if nxt is not None:
            qk = stage_qk(nxt, b_n)
        st, o2 = stage_recur(s, st, q_in, scores, upds, decays, hc // 2, hc)
        stage_out(s, o1 + o2)
    st_ref[...] = st


def _gla(proj, glr, wup_pad, b_alpha2, norm_w2, batch, seq, t_rows=2048, n_parts=8):
    m = proj.shape[0]
    nt = seq // t_rows
    row = lambda b, h, t: b * nt + t
    return pl.pallas_call(
        functools.partial(_gla_kernel, t_rows=t_rows, n_parts=n_parts),
        grid=(batch, GLA_HEADS, nt),
        in_specs=[
            pl.BlockSpec((t_rows, GLA_DK), lambda b, h, t: (row(b, h, t), Q_OFF // GLA_DK + h)),
            pl.BlockSpec((t_rows, GLA_DK), lambda b, h, t: (row(b, h, t), K_OFF // GLA_DK + h)),
            pl.BlockSpec((t_rows, GLA_DV), lambda b, h, t: (row(b, h, t), V_OFF // GLA_DV + h)),
            pl.BlockSpec((t_rows, GLA_DV), lambda b, h, t: (row(b, h, t), GO_OFF // GLA_DV + h)),
            pl.BlockSpec((t_rows, LANE), lambda b, h, t: (row(b, h, t), 0)),
            pl.BlockSpec((LANE, GLA_DK), lambda b, h, t: (0, h)),
            pl.BlockSpec((1, GLA_DK), lambda b, h, t: (0, h)),
            pl.BlockSpec((1, GLA_DV), lambda b, h, t: (0, 0)),
        ],
        out_specs=pl.BlockSpec((t_rows, GLA_DV), lambda b, h, t: (row(b, h, t), h)),
        out_shape=jax.ShapeDtypeStruct((m, GLA_HEADS * GLA_DV), BF16),
        scratch_shapes=[pltpu.VMEM((GLA_DV, GLA_DK), F32)],
        compiler_params=pltpu.CompilerParams(
            dimension_semantics=("arbitrary", "arbitrary", "arbitrary"),
            vmem_limit_bytes=VMEM_LIMIT),
        name="gla",
    )(proj, proj, proj, proj, glr, wup_pad, b_alpha2, norm_w2)


def _alibi_slopes():
    h = np.arange(1, SWA_HEADS + 1, dtype=np.float32)
    return [float(s) for s in (2.0 ** (-8.0 * h / SWA_HEADS)).astype(np.float32)]


MASKED = -1e30
ONES_ROWS = 16
SWA_LOOKAHEAD = 5


def _swa_kernel(sink_ref, slope_ref, q_ref, kp_ref, kc_ref, vp_ref, vc_ref, o_ref,
                tbl_ref, *, q_blocks):
    w = WINDOW
    n = pl.program_id(1)

    @pl.when((pl.program_id(0) == 0) & (n == 0))
    def _():
        s = lax.broadcasted_iota(jnp.int32, (2 * w, w), 0)
        t = lax.broadcasted_iota(jnp.int32, (2 * w, w), 1)
        dist = t - s + w
        band = (dist >= 0) & (dist < w)
        ndist = -dist.astype(F32)

        def fill(h, carry):
            bias = (slope_ref[h] * LOG2E) * ndist
            tbl_ref[0, h] = jnp.where(band & (s >= w), bias, MASKED)
            tbl_ref[1, h] = jnp.where(band, bias, MASKED)
            return carry

        lax.fori_loop(0, SWA_HEADS, fill, 0)

    lane = lax.broadcasted_iota(jnp.int32, (2 * w, LANE), 1)
    low = lane < SWA_HD
    lane_q = lax.broadcasted_iota(jnp.int32, (w, LANE), 1)
    low_q = lane_q < SWA_HD
    ones = jnp.ones((ONES_ROWS, 2 * w), BF16)

    kall = jnp.concatenate([kp_ref[...], kc_ref[...]], axis=0)
    vall = jnp.concatenate([vp_ref[...], vc_ref[...]], axis=0)
    n_pairs = SWA_HEADS // 2
    pairs_per_kv = SWA_GROUP // 2

    def window(blk):
        kfull = kall[blk * w:(blk + 2) * w, :]
        v_t = vall[blk * w:(blk + 2) * w, :].astype(F32).T.astype(BF16)
        kdups = []
        for kvp in range(SWA_KV_HEADS // 2):
            kcol = kfull[:, kvp * LANE:(kvp + 1) * LANE].astype(F32) * LOG2E
            krol = pltpu.roll(kcol, SWA_HD, 1)
            kdups.append(jnp.where(low, kcol, krol).astype(BF16))
            kdups.append(jnp.where(low, krol, kcol).astype(BF16))
        vaugs = [jnp.concatenate([v_t[kv * SWA_HD:(kv + 1) * SWA_HD, :], ones], axis=0)
                 for kv in range(SWA_KV_HEADS)]
        return kdups, vaugs

    windows = [window(blk) for blk in range(q_blocks)]

    def logits_t(blk, p_idx):
        qp = (q_ref[blk * w:(blk + 1) * w, p_idx * LANE:(p_idx + 1) * LANE]
              * (SWA_HD ** -0.5))
        qw = jnp.concatenate([jnp.where(low_q, qp, 0.0),
                              jnp.where(low_q, 0.0, qp)], axis=0)
        return lax.dot_general(windows[blk][0][p_idx // pairs_per_kv], qw,
                               (((1,), (1,)), ((), ())),
                               preferred_element_type=F32)

    def finish(blk, p_idx, lg):
        var = jnp.minimum(n, 1) if blk == 0 else 1
        ps, ms = [], []
        for e in range(2):
            h = 2 * p_idx + e
            l = lg[:, e * w:(e + 1) * w] + tbl_ref[var, h]
            mx = jnp.maximum(jnp.max(l, axis=0, keepdims=True), sink_ref[h] * LOG2E)
            ps.append(jnp.exp2(l - mx).astype(BF16))
            ms.append(mx)
        ot = jnp.dot(windows[blk][1][p_idx // pairs_per_kv], jnp.concatenate(ps, axis=1),
                     preferred_element_type=F32)
        halves = []
        for e in range(2):
            h = 2 * p_idx + e
            den = (ot[SWA_HD:SWA_HD + 1, e * w:(e + 1) * w]
                   + jnp.exp2(sink_ref[h] * LOG2E - ms[e]))
            halves.append(ot[0:SWA_HD, e * w:(e + 1) * w] * (1.0 / den))
        o_ref[blk * w:(blk + 1) * w, p_idx * LANE:(p_idx + 1) * LANE] = (
            jnp.concatenate(halves, axis=0).T.astype(BF16))

    units = [(blk, p) for blk in range(q_blocks) for p in range(n_pairs)]
    pending = {u: logits_t(*u) for u in units[:SWA_LOOKAHEAD]}
    for idx, u in enumerate(units):
        finish(*u, pending.pop(u))
        if idx + SWA_LOOKAHEAD < len(units):
            nu = units[idx + SWA_LOOKAHEAD]
            pending[nu] = logits_t(*nu)


def _swa(proj, sinks, batch, seq, q_blocks=4):
    m = proj.shape[0]
    w = WINDOW
    tq = q_blocks * w
    ns = seq // tq
    n_heads_cols = SWA_HEADS * SWA_HD
    kvw = SWA_KV_HEADS * SWA_HD
    cur = lambda b, n: b * ns + n
    prev = lambda b, n: (b * ns + n) * q_blocks - jnp.minimum(n, 1)
    return pl.pallas_call(
        functools.partial(_swa_kernel, q_blocks=q_blocks),
        grid=(batch, ns),
        in_specs=[
            pl.BlockSpec(memory_space=pltpu.SMEM),
            pl.BlockSpec(memory_space=pltpu.SMEM),
            pl.BlockSpec((tq, n_heads_cols), lambda b, n: (cur(b, n), SQ_OFF // n_heads_cols)),
            pl.BlockSpec((w, kvw), lambda b, n: (prev(b, n), SK_OFF // kvw)),
            pl.BlockSpec((tq, kvw), lambda b, n: (cur(b, n), SK_OFF // kvw)),
            pl.BlockSpec((w, kvw), lambda b, n: (prev(b, n), SV_OFF // kvw)),
            pl.BlockSpec((tq, kvw), lambda b, n: (cur(b, n), SV_OFF // kvw)),
        ],
        out_specs=pl.BlockSpec((tq, n_heads_cols), lambda b, n: (cur(b, n), 0)),
        out_shape=jax.ShapeDtypeStruct((m, n_heads_cols), BF16),
        scratch_shapes=[pltpu.VMEM((2, SWA_HEADS, 2 * w, w), F32)],
        compiler_params=pltpu.CompilerParams(
            dimension_semantics=("arbitrary", "arbitrary"),
            vmem_limit_bytes=VMEM_LIMIT),
        name="swa",
    )(sinks, jnp.asarray(_alibi_slopes(), F32), proj, proj, proj, proj, proj)


def _merge_kernel(oa_ref, ob_ref, gg_ref, gs_ref, x_ref, wg_ref, ws_ref, wo_ref,
                  g_ref, b_ref, o_ref, z_ref, *, n_split):
    i = pl.program_id(0)
    slot = i % 2

    @pl.when(i == 0)
    def _():
        z_ref[1] = jnp.zeros(z_ref.shape[1:], F32)

    ln = _layer_norm(z_ref[1 - slot], g_ref[...], b_ref[...])
    o_ref[...] = ln
    rows = oa_ref.shape[0] // n_split
    groups = [slice(s * rows, (s + 1) * rows) for s in range(n_split)]
    merged = []
    for r in groups:
        ya = jnp.dot(oa_ref[r, :], wg_ref[...], preferred_element_type=F32)
        yb = jnp.dot(ob_ref[r, :], ws_ref[...], preferred_element_type=F32)
        merged.append(jax.nn.sigmoid(gg_ref[r, :].astype(F32)) * ya
                      + jax.nn.sigmoid(gs_ref[r, :].astype(F32)) * yb)
    zeros = _zero_after(ln)
    anchor = functools.reduce(lambda a, b: a + b, [zeros[r] for r in groups])
    for s, r in enumerate(groups):
        m_s = merged[s]
        if s == n_split - 1:
            m_s = jnp.concatenate([m_s[:, :LANE] + anchor, m_s[:, LANE:]], axis=1)
        mix = jnp.dot(m_s.astype(BF16), wo_ref[...], preferred_element_type=F32)
        z_ref[slot, r, :] = ALPHA * x_ref[r, :] + mix


def _merge(o_a, o_b, proj, x2, wg, ws, wo, ln_g, ln_b, tm=256, n_split=1):
    m = x2.shape[0]
    nt = m // tm
    wspec = pl.BlockSpec((D_MODEL, D_MODEL), lambda i: (0, 0),
                         pipeline_mode=pl.Buffered(1))
    vspec = pl.BlockSpec((1, D_MODEL), lambda i: (0, 0))
    src = lambda i: jnp.minimum(i, nt - 1)
    return pl.pallas_call(
        functools.partial(_merge_kernel, n_split=n_split),
        grid=(nt + 1,),
        in_specs=[
            pl.BlockSpec((tm, D_MODEL), lambda i: (src(i), 0)),
            pl.BlockSpec((tm, D_MODEL), lambda i: (src(i), 0)),
            pl.BlockSpec((tm, D_MODEL), lambda i: (src(i), GG_OFF // D_MODEL)),
            pl.BlockSpec((tm, D_MODEL), lambda i: (src(i), GS_OFF // D_MODEL)),
            pl.BlockSpec((tm, D_MODEL), lambda i: (src(i), 0)),
            wspec, wspec, wspec, vspec, vspec,
        ],
        out_specs=pl.BlockSpec((tm, D_MODEL), lambda i: (jnp.maximum(i - 1, 0), 0)),
        out_shape=jax.ShapeDtypeStruct((m, D_MODEL), F32),
        scratch_shapes=[pltpu.VMEM((2, tm, D_MODEL), F32)],
        compiler_params=pltpu.CompilerParams(
            dimension_semantics=("arbitrary",),
            vmem_limit_bytes=VMEM_LIMIT),
        name="merge",
    )(o_a, o_b, proj, proj, x2, wg, ws, wo, ln_g, ln_b)


def _ffn_kernel(x_ref, wu_ref, wd_ref, g_ref, b_ref, o_ref, xb_ref):
    j = pl.program_id(1)

    @pl.when(j == 0)
    def _():
        xb_ref[...] = x_ref[...].astype(BF16)
        o_ref[...] = jnp.zeros_like(o_ref)

    h = jnp.dot(xb_ref[...], wu_ref[...], preferred_element_type=F32)
    h = jnp.maximum(h, 0.0)
    h = (h * h).astype(BF16)
    o_ref[...] += jnp.dot(h, wd_ref[...], preferred_element_type=F32)

    @pl.when(j == pl.num_programs(1) - 1)
    def _():
        o_ref[...] = _layer_norm(ALPHA * x_ref[...] + o_ref[...],
                                 g_ref[...], b_ref[...])


def _ffn(x1, wu, wd, ln_g, ln_b, tm=512, tf=2048):
    m = x1.shape[0]
    vspec = pl.BlockSpec((1, D_MODEL), lambda i, j: (0, 0))
    return pl.pallas_call(
        _ffn_kernel,
        grid=(m // tm, D_FF // tf),
        in_specs=[
            pl.BlockSpec((tm, D_MODEL), lambda i, j: (i, 0)),
            pl.BlockSpec((D_MODEL, tf), lambda i, j: (0, j)),
            pl.BlockSpec((tf, D_MODEL), lambda i, j: (j, 0)),
            vspec, vspec,
        ],
        out_specs=pl.BlockSpec((tm, D_MODEL), lambda i, j: (i, 0)),
        out_shape=jax.ShapeDtypeStruct((m, D_MODEL), F32),
        scratch_shapes=[pltpu.VMEM((tm, D_MODEL), BF16)],
        compiler_params=pltpu.CompilerParams(
            dimension_semantics=("arbitrary", "arbitrary"),
            vmem_limit_bytes=FFN_VMEM_LIMIT),
        name="ffn",
    )(x1, wu, wd, ln_g, ln_b)


def kernel(x, w_in, w_alpha_up, b_alpha, gla_norm_w, attn_sinks, w_branch_gla,
           w_branch_swa, w_out, ln1_g, ln1_b, w_ff_up, w_ff_down, ln2_g, ln2_b):
    batch, seq, d = x.shape
    x2 = x.reshape(batch * seq, d)

    w_tail = w_in[:, 6160:12816].astype(BF16)
    w_main = jnp.concatenate(
        [w_in[:, 0:6144].astype(BF16), w_tail[:, 0:2048], w_tail[:, 2560:4608],
         w_tail[:, 4608:6656], w_tail[:, 2048:2560]], axis=1)
    w_lr = jnp.pad(w_in[:, 6144:6160], ((0, 0), (0, LANE - GLA_RANK))).astype(BF16)
    wup_pad = jnp.pad(w_alpha_up, ((0, LANE - GLA_RANK), (0, 0))).astype(BF16)

    proj, glr = _proj(x2, w_main, w_lr)
    o_a = _gla(proj, glr, wup_pad, b_alpha.reshape(1, -1).astype(F32),
               gla_norm_w.reshape(1, -1).astype(F32), batch, seq)
    o_b = _swa(proj, attn_sinks.astype(F32), batch, seq)
    x1 = _merge(o_a, o_b, proj, x2, w_branch_gla.astype(BF16),
                w_branch_swa.astype(BF16), w_out.astype(BF16),
                ln1_g.reshape(1, -1), ln1_b.reshape(1, -1))
    out = _ffn(x1, w_ff_up.astype(BF16), w_ff_down.astype(BF16),
               ln2_g.reshape(1, -1), ln2_b.reshape(1, -1))
    return out.reshape(batch, seq, d)
```

```python
import functools

import jax
import jax.numpy as jnp
import numpy as np
from jax import lax
from jax.experimental import pallas as pl
from jax.experimental.pallas import tpu as pltpu

F32 = jnp.float32
BF16 = jnp.bfloat16

D_MODEL = 2048
GLA_HEADS = 4
GLA_DK = 256
GLA_DV = 512
GLA_RANK = 16
GLA_TAU = 16.0
GLA_CHUNK = 64
SWA_HEADS = 32
SWA_KV_HEADS = 4
SWA_HD = 64
SWA_GROUP = 8
WINDOW = 128
D_FF = 4 * D_MODEL
ALPHA = 2.0 ** 0.25
LN_EPS = 1e-5
RMS_EPS = 1e-6
LOG2E = 1.4426950408889634

Q_OFF, K_OFF, V_OFF, GO_OFF, SQ_OFF, GG_OFF, GS_OFF, SK_OFF, SV_OFF = (
    0, 1024, 2048, 4096, 6144, 8192, 10240, 12288, 12544)
N_PROJ = 12800
LANE = 128
MXU_COLS = 256

VMEM_LIMIT = 56 * 1024 * 1024
FFN_VMEM_LIMIT = 60 * 1024 * 1024


def _layer_norm(z, g, b):
    mu = jnp.mean(z, axis=-1, keepdims=True)
    zc = z - mu
    var = jnp.mean(zc * zc, axis=-1, keepdims=True)
    return zc * lax.rsqrt(var + LN_EPS) * g + b


def _proj_kernel(x_ref, w_ref, wlr_ref, o_ref, lr_ref, xb_ref):
    j = pl.program_id(1)

    @pl.when(j == 0)
    def _():
        xb = x_ref[...].astype(BF16)
        xb_ref[...] = xb
        o_ref[...] = jnp.dot(xb, w_ref[...], preferred_element_type=F32).astype(BF16)
        lr_ref[...] = jnp.dot(xb, wlr_ref[...], preferred_element_type=F32)

    @pl.when(j > 0)
    def _():
        o_ref[...] = jnp.dot(xb_ref[...], w_ref[...],
                             preferred_element_type=F32).astype(BF16)


def _proj(x2, w_main, w_lr, tm=1024, tn=2560):
    m = x2.shape[0]
    return pl.pallas_call(
        _proj_kernel,
        grid=(m // tm, N_PROJ // tn),
        in_specs=[
            pl.BlockSpec((tm, D_MODEL), lambda i, j: (i, 0)),
            pl.BlockSpec((D_MODEL, tn), lambda i, j: (0, j)),
            pl.BlockSpec((D_MODEL, LANE), lambda i, j: (0, 0)),
        ],
        out_specs=[
            pl.BlockSpec((tm, tn), lambda i, j: (i, j)),
            pl.BlockSpec((tm, LANE), lambda i, j: (i, 0)),
        ],
        out_shape=[
            jax.ShapeDtypeStruct((m, N_PROJ), BF16),
            jax.ShapeDtypeStruct((m, LANE), F32),
        ],
        scratch_shapes=[pltpu.VMEM((tm, D_MODEL), BF16)],
        compiler_params=pltpu.CompilerParams(
            dimension_semantics=("arbitrary", "arbitrary"),
            vmem_limit_bytes=VMEM_LIMIT),
        name="proj",
    )(x2, w_main, w_lr)


def _gla_kernel(q_ref, k_ref, v_ref, go_ref, lr_ref, wup_ref, ba_ref, nw_ref,
                o_ref, st_ref, *, t_rows, n_parts):
    c = GLA_CHUNK
    cs = 2 * LANE

    @pl.when(pl.program_id(2) == 0)
    def _():
        st_ref[...] = jnp.zeros_like(st_ref)

    row = lax.broadcasted_iota(jnp.int32, (cs, cs), 0)
    col = lax.broadcasted_iota(jnp.int32, (cs, cs), 1)
    tri = jnp.where((row >= col) & (row // c == col // c), 1.0, 0.0).astype(BF16)
    r64 = lax.broadcasted_iota(jnp.int32, (c, c), 0)
    c64 = lax.broadcasted_iota(jnp.int32, (c, c), 1)
    causal = r64 >= c64

    half = t_rows // n_parts
    hc = half // c

    def stage_z(s):
        return jnp.dot(lr_ref[s:s + half, :].astype(BF16), wup_ref[...],
                       preferred_element_type=F32) + ba_ref[...]

    def stage_cumsum(z):
        softplus2 = jnp.log2(1.0 + jnp.exp2(jnp.abs(z) * (-LOG2E)))
        la = (jnp.minimum(z, 0.0) * LOG2E - softplus2) * (1.0 / GLA_TAU)
        la_hi = la.astype(BF16)
        la_lo = (la - la_hi.astype(F32)).astype(BF16)
        return jnp.concatenate(
            [jnp.dot(tri, la_hi[r:r + cs], preferred_element_type=F32)
             + jnp.dot(tri, la_lo[r:r + cs], preferred_element_type=F32)
             for r in range(0, half, cs)], axis=0)

    def stage_qk(s, b):
        b_last = [b[(i + 1) * c - 1:(i + 1) * c, :] for i in range(hc)]
        bl = jnp.concatenate([jnp.broadcast_to(r, (c, GLA_DK)) for r in b_last], axis=0)
        q = q_ref[s:s + half, :].astype(F32)
        k = k_ref[s:s + half, :].astype(F32)
        q_in = (q * jnp.exp2(b)).astype(BF16)
        k_in = (k * jnp.exp2(-b)).astype(BF16)
        k_dec = (k * jnp.exp2(bl - b)).astype(BF16)
        return q_in, k_in, k_dec, [jnp.exp2(r) for r in b_last]

    def stage_mm(s, q_in, k_in, k_dec):
        sub = [slice(i * c, (i + 1) * c) for i in range(hc)]
        scores = [
            jnp.where(causal,
                      lax.dot_general(q_in[r], k_in[r], (((1,), (1,)), ((), ())),
                                      preferred_element_type=F32),
                      0.0).astype(BF16)
            for r in sub]
        upds = [lax.dot_general(v_ref[s + r.start:s + r.stop, :], k_dec[r],
                                (((0,), (0,)), ((), ())),
                                preferred_element_type=F32)
                for r in sub]
        return scores, upds

    def stage_recur(s, st, q_in, scores, upds, decays, lo, hi):
        outs = []
        for i in range(lo, hi):
            r = slice(i * c, (i + 1) * c)
            outs.append(
                jnp.dot(scores[i], v_ref[s + r.start:s + r.stop, :],
                        preferred_element_type=F32)
                + lax.dot_general(q_in[r], st.astype(BF16), (((1,), (1,)), ((), ())),
                                  preferred_element_type=F32))
            st = st * decays[i] + upds[i]
        return st, outs

    def stage_out(s, outs):
        o = jnp.concatenate(outs, axis=0)
        rms = lax.rsqrt(jnp.mean(o * o, axis=-1, keepdims=True) + RMS_EPS * GLA_DK)
        g = go_ref[s:s + half, :].astype(F32)
        o_ref[s:s + half, :] = (o * rms * nw_ref[...]
                                * (g * jax.nn.sigmoid(g))).astype(BF16)

    starts = list(range(0, t_rows, half))
    st = st_ref[...]
    qk = stage_qk(starts[0], stage_cumsum(stage_z(starts[0])))
    for p, s in enumerate(starts):
        nxt = starts[p + 1] if p + 1 < len(starts) else None
        q_in, k_in, k_dec, decays = qk
        if nxt is not None:
            z_n = stage_z(nxt)
        scores, upds = stage_mm(s, q_in, k_in, k_dec)
        if nxt is not None:
            b_n = stage_cumsum(z_n)
        st, o1 = stage_recur(s, st, q_in, scores, upds, decays, 0, hc // 2)
        if nxt is not None:
            qk = stage_qk(nxt, b_n)
        st, o2 = stage_recur(s, st, q_in, scores, upds, decays, hc // 2, hc)
        stage_out(s, o1 + o2)
    st_ref[...] = st


def _gla(proj, glr, wup_pad, b_alpha2, norm_w2, batch, seq, t_rows=2048, n_parts=8):
    m = proj.shape[0]
    nt = seq // t_rows
    row = lambda b, h, t: b * nt + t
    return pl.pallas_call(
        functools.partial(_gla_kernel, t_rows=t_rows, n_parts=n_parts),
        grid=(batch, GLA_HEADS, nt),
        in_specs=[
            pl.BlockSpec((t_rows, GLA_DK), lambda b, h, t: (row(b, h, t), Q_OFF // GLA_DK + h)),
            pl.BlockSpec((t_rows, GLA_DK), lambda b, h, t: (row(b, h, t), K_OFF // GLA_DK + h)),
            pl.BlockSpec((t_rows, GLA_DV), lambda b, h, t: (row(b, h, t), V_OFF // GLA_DV + h)),
            pl.BlockSpec((t_rows, GLA_DV), lambda b, h, t: (row(b, h, t), GO_OFF // GLA_DV + h)),
            pl.BlockSpec((t_rows, LANE), lambda b, h, t: (row(b, h, t), 0)),
            pl.BlockSpec((LANE, GLA_DK), lambda b, h, t: (0, h)),
            pl.BlockSpec((1, GLA_DK), lambda b, h, t: (0, h)),
            pl.BlockSpec((1, GLA_DV), lambda b, h, t: (0, 0)),
        ],
        out_specs=pl.BlockSpec((t_rows, GLA_DV), lambda b, h, t: (row(b, h, t), h)),
        out_shape=jax.ShapeDtypeStruct((m, GLA_HEADS * GLA_DV), BF16),
        scratch_shapes=[pltpu.VMEM((GLA_DV, GLA_DK), F32)],
        compiler_params=pltpu.CompilerParams(
            dimension_semantics=("arbitrary", "arbitrary", "arbitrary"),
            vmem_limit_bytes=VMEM_LIMIT),
        name="gla",
    )(proj, proj, proj, proj, glr, wup_pad, b_alpha2, norm_w2)


def _alibi_slopes():
    h = np.arange(1, SWA_HEADS + 1, dtype=np.float32)
    return [float(s) for s in (2.0 ** (-8.0 * h / SWA_HEADS)).astype(np.float32)]


MASKED = -1e30
ONES_ROWS = 16
SWA_LOOKAHEAD = 5


def _swa_fill_tables(slope_ref, tbl_ref):
    w = WINDOW
    s = lax.broadcasted_iota(jnp.int32, (2 * w, w), 0)
    t = lax.broadcasted_iota(jnp.int32, (2 * w, w), 1)
    dist = t - s + w
    band = (dist >= 0) & (dist < w)
    ndist = -dist.astype(F32)

    def fill(h, carry):
        bias = (slope_ref[h] * LOG2E) * ndist
        tbl_ref[0, h] = jnp.where(band & (s >= w), bias, MASKED)
        tbl_ref[1, h] = jnp.where(band, bias, MASKED)
        return carry

    lax.fori_loop(0, SWA_HEADS, fill, 0)


def _swa_blocks(n, sink_ref, q_ref, kp_ref, kc_ref, vp_ref, vc_ref, o_ref, tbl_ref,
                q_blocks, between=None):
    w = WINDOW
    lane = lax.broadcasted_iota(jnp.int32, (2 * w, LANE), 1)
    low = lane < SWA_HD
    lane_q = lax.broadcasted_iota(jnp.int32, (w, LANE), 1)
    low_q = lane_q < SWA_HD
    ones = jnp.ones((ONES_ROWS, 2 * w), BF16)

    kall = jnp.concatenate([kp_ref[...], kc_ref[...]], axis=0)
    vall = jnp.concatenate([vp_ref[...], vc_ref[...]], axis=0)
    n_pairs = SWA_HEADS // 2
    pairs_per_kv = SWA_GROUP // 2

    def window(blk):
        kfull = kall[blk * w:(blk + 2) * w, :]
        v_t = vall[blk * w:(blk + 2) * w, :].astype(F32).T.astype(BF16)
        kdups = []
        for kvp in range(SWA_KV_HEADS // 2):
            kcol = kfull[:, kvp * LANE:(kvp + 1) * LANE].astype(F32) * LOG2E
            krol = pltpu.roll(kcol, SWA_HD, 1)
            kdups.append(jnp.where(low, kcol, krol).astype(BF16))
            kdups.append(jnp.where(low, krol, kcol).astype(BF16))
        vaugs = [jnp.concatenate([v_t[kv * SWA_HD:(kv + 1) * SWA_HD, :], ones], axis=0)
                 for kv in range(SWA_KV_HEADS)]
        return kdups, vaugs

    windows = [window(blk) for blk in range(q_blocks)]

    def logits_t(blk, p_idx):
        qp = (q_ref[blk * w:(blk + 1) * w, p_idx * LANE:(p_idx + 1) * LANE]
              * (SWA_HD ** -0.5))
        qw = jnp.concatenate([jnp.where(low_q, qp, 0.0),
                              jnp.where(low_q, 0.0, qp)], axis=0)
        return lax.dot_general(windows[blk][0][p_idx // pairs_per_kv], qw,
                               (((1,), (1,)), ((), ())),
                               preferred_element_type=F32)

    def finish(blk, p_idx, lg):
        var = jnp.minimum(n, 1) if blk == 0 else 1
        ps, ms = [], []
        for e in range(2):
            h = 2 * p_idx + e
            l = lg[:, e * w:(e + 1) * w] + tbl_ref[var, h]
            mx = jnp.maximum(jnp.max(l, axis=0, keepdims=True), sink_ref[h] * LOG2E)
            ps.append(jnp.exp2(l - mx).astype(BF16))
            ms.append(mx)
        ot = jnp.dot(windows[blk][1][p_idx // pairs_per_kv], jnp.concatenate(ps, axis=1),
                     preferred_element_type=F32)
        halves = []
        for e in range(2):
            h = 2 * p_idx + e
            den = (ot[SWA_HD:SWA_HD + 1, e * w:(e + 1) * w]
                   + jnp.exp2(sink_ref[h] * LOG2E - ms[e]))
            halves.append(ot[0:SWA_HD, e * w:(e + 1) * w] * (1.0 / den))
        o_ref[blk * w:(blk + 1) * w, p_idx * LANE:(p_idx + 1) * LANE] = (
            jnp.concatenate(halves, axis=0).T.astype(BF16))

    units = [(blk, p) for blk in range(q_blocks) for p in range(n_pairs)]
    pending = {u: logits_t(*u) for u in units[:SWA_LOOKAHEAD]}
    for idx, u in enumerate(units):
        finish(*u, pending.pop(u))
        if idx + SWA_LOOKAHEAD < len(units):
            nu = units[idx + SWA_LOOKAHEAD]
            pending[nu] = logits_t(*nu)
        if between is not None:
            between(idx)


def _attn_merge_kernel(sink_ref, slope_ref, q_ref, kp_ref, kc_ref, vp_ref, vc_ref,
                       oa_ref, gg_ref, gs_ref, x_ref, wg_ref, ws_ref, wo_ref,
                       g_ref, b_ref, o_ref, tbl_ref, ob_ref, *, q_blocks):
    n = pl.program_id(1)

    @pl.when((pl.program_id(0) == 0) & (n == 0))
    def _():
        _swa_fill_tables(slope_ref, tbl_ref)

    n_units = q_blocks * (SWA_HEADS // 2)
    n_chunks = D_MODEL // MXU_COLS
    every = n_units // n_chunks
    ya_chunks = []

    def between(idx):
        if (idx + 1) % every == 0:
            c0 = len(ya_chunks) * MXU_COLS
            ya_chunks.append(jnp.dot(oa_ref[...], wg_ref[:, c0:c0 + MXU_COLS],
                                     preferred_element_type=F32))

    _swa_blocks(n, sink_ref, q_ref, kp_ref, kc_ref, vp_ref, vc_ref, ob_ref, tbl_ref,
                q_blocks, between)
    ya = jnp.concatenate(ya_chunks, axis=1)
    yb = jnp.dot(ob_ref[...], ws_ref[...], preferred_element_type=F32)
    merged = (jax.nn.sigmoid(gg_ref[...].astype(F32)) * ya
              + jax.nn.sigmoid(gs_ref[...].astype(F32)) * yb)
    mix = jnp.dot(merged.astype(BF16), wo_ref[...], preferred_element_type=F32)
    o_ref[...] = _layer_norm(ALPHA * x_ref[...] + mix, g_ref[...], b_ref[...])


def _attn_merge(proj, sinks, o_a, x2, wg, ws, wo, ln_g, ln_b, batch, seq, q_blocks=2):
    m = proj.shape[0]
    w = WINDOW
    tq = q_blocks * w
    ns = seq // tq
    kvw = SWA_KV_HEADS * SWA_HD
    cur = lambda b, n: b * ns + n
    prev = lambda b, n: (b * ns + n) * q_blocks - jnp.minimum(n, 1)
    rows = lambda col: pl.BlockSpec((tq, D_MODEL), lambda b, n: (cur(b, n), col))
    wspec = pl.BlockSpec((D_MODEL, D_MODEL), lambda b, n: (0, 0),
                         pipeline_mode=pl.Buffered(1))
    vspec = pl.BlockSpec((1, D_MODEL), lambda b, n: (0, 0))
    return pl.pallas_call(
        functools.partial(_attn_merge_kernel, q_blocks=q_blocks),
        grid=(batch, ns),
        in_specs=[
            pl.BlockSpec(memory_space=pltpu.SMEM),
            pl.BlockSpec(memory_space=pltpu.SMEM),
            rows(SQ_OFF // D_MODEL),
            pl.BlockSpec((w, kvw), lambda b, n: (prev(b, n), SK_OFF // kvw)),
            pl.BlockSpec((tq, kvw), lambda b, n: (cur(b, n), SK_OFF // kvw)),
            pl.BlockSpec((w, kvw), lambda b, n: (prev(b, n), SV_OFF // kvw)),
            pl.BlockSpec((tq, kvw), lambda b, n: (cur(b, n), SV_OFF // kvw)),
            rows(0), rows(GG_OFF // D_MODEL), rows(GS_OFF // D_MODEL), rows(0),
            wspec, wspec, wspec, vspec, vspec,
        ],
        out_specs=rows(0),
        out_shape=jax.ShapeDtypeStruct((m, D_MODEL), F32),
        scratch_shapes=[pltpu.VMEM((2, SWA_HEADS, 2 * w, w), F32),
                        pltpu.VMEM((tq, D_MODEL), BF16)],
        compiler_params=pltpu.CompilerParams(
            dimension_semantics=("arbitrary", "arbitrary"),
            vmem_limit_bytes=FFN_VMEM_LIMIT),
        name="attn_merge",
    )(sinks, jnp.asarray(_alibi_slopes(), F32), proj, proj, proj, proj, proj,
      o_a, proj, proj, x2, wg, ws, wo, ln_g, ln_b)


def _ffn_kernel(x_ref, wu_ref, wd_ref, g_ref, b_ref, o_ref, xb_ref):
    j = pl.program_id(1)

    @pl.when(j == 0)
    def _():
        xb_ref[...] = x_ref[...].astype(BF16)
        o_ref[...] = jnp.zeros_like(o_ref)

    h = jnp.dot(xb_ref[...], wu_ref[...], preferred_element_type=F32)
    h = jnp.maximum(h, 0.0)
    h = (h * h).astype(BF16)
    o_ref[...] += jnp.dot(h, wd_ref[...], preferred_element_type=F32)

    @pl.when(j == pl.num_programs(1) - 1)
    def _():
        o_ref[...] = _layer_norm(ALPHA * x_ref[...] + o_ref[...],
                                 g_ref[...], b_ref[...])


def _ffn(x1, wu, wd, ln_g, ln_b, tm=512, tf=2048):
    m = x1.shape[0]
    vspec = pl.BlockSpec((1, D_MODEL), lambda i, j: (0, 0))
    return pl.pallas_call(
        _ffn_kernel,
        grid=(m // tm, D_FF // tf),
        in_specs=[
            pl.BlockSpec((tm, D_MODEL), lambda i, j: (i, 0)),
            pl.BlockSpec((D_MODEL, tf), lambda i, j: (0, j)),
            pl.BlockSpec((tf, D_MODEL), lambda i, j: (j, 0)),
            vspec, vspec,
        ],
        out_specs=pl.BlockSpec((tm, D_MODEL), lambda i, j: (i, 0)),
        out_shape=jax.ShapeDtypeStruct((m, D_MODEL), F32),
        scratch_shapes=[pltpu.VMEM((tm, D_MODEL), BF16)],
        compiler_params=pltpu.CompilerParams(
            dimension_semantics=("arbitrary", "arbitrary"),
            vmem_limit_bytes=FFN_VMEM_LIMIT),
        name="ffn",
    )(x1, wu, wd, ln_g, ln_b)


def kernel(x, w_in, w_alpha_up, b_alpha, gla_norm_w, attn_sinks, w_branch_gla,
           w_branch_swa, w_out, ln1_g, ln1_b, w_ff_up, w_ff_down, ln2_g, ln2_b):
    batch, seq, d = x.shape
    x2 = x.reshape(batch * seq, d)

    w_tail = lax.optimization_barrier(w_in[:, 6160:12816].astype(BF16))
    w_main = jnp.concatenate(
        [w_in[:, 0:6144].astype(BF16), w_tail[:, 0:2048], w_tail[:, 2560:4608],
         w_tail[:, 4608:6656], w_tail[:, 2048:2560]], axis=1)
    w_lr = jnp.pad(w_in[:, 6144:6160], ((0, 0), (0, LANE - GLA_RANK))).astype(BF16)
    wup_pad = jnp.pad(w_alpha_up, ((0, LANE - GLA_RANK), (0, 0))).astype(BF16)

    proj, glr = _proj(x2, w_main, w_lr)
    o_a = _gla(proj, glr, wup_pad, b_alpha.reshape(1, -1).astype(F32),
               gla_norm_w.reshape(1, -1).astype(F32), batch, seq)
    x1 = _attn_merge(proj, attn_sinks.astype(F32), o_a, x2, w_branch_gla.astype(BF16),
                     w_branch_swa.astype(BF16), w_out.astype(BF16),
                     ln1_g.reshape(1, -1), ln1_b.reshape(1, -1), batch, seq)
    out = _ffn(x1, w_ff_up.astype(BF16), w_ff_down.astype(BF16),
               ln2_g.reshape(1, -1), ln2_b.reshape(1, -1))
    return out.reshape(batch, seq, d)
```

```python
import functools

import jax
import jax.numpy as jnp
import numpy as np
from jax import lax
from jax.experimental import pallas as pl
from jax.experimental.pallas import tpu as pltpu

F32 = jnp.float32
BF16 = jnp.bfloat16

D_MODEL = 2048
GLA_HEADS = 4
GLA_DK = 256
GLA_DV = 512
GLA_RANK = 16
GLA_TAU = 16.0
GLA_CHUNK = 64
SWA_HEADS = 32
SWA_KV_HEADS = 4
SWA_HD = 64
SWA_GROUP = 8
WINDOW = 128
D_FF = 4 * D_MODEL
ALPHA = 2.0 ** 0.25
LN_EPS = 1e-5
RMS_EPS = 1e-6
LOG2E = 1.4426950408889634

Q_OFF, K_OFF, V_OFF, GO_OFF, SQ_OFF, GG_OFF, GS_OFF, SK_OFF, SV_OFF = (
    0, 1024, 2048, 4096, 6144, 8192, 10240, 12288, 12544)
N_PROJ = 12800
LANE = 128
MXU_COLS = 256

VMEM_LIMIT = 56 * 1024 * 1024
FFN_VMEM_LIMIT = 60 * 1024 * 1024


def _layer_norm(z, g, b):
    mu = jnp.mean(z, axis=-1, keepdims=True)
    zc = z - mu
    var = jnp.mean(zc * zc, axis=-1, keepdims=True)
    return zc * lax.rsqrt(var + LN_EPS) * g + b


def _proj_kernel(x_ref, w_ref, wlr_ref, o_ref, lr_ref, xb_ref):
    j = pl.program_id(1)

    @pl.when(j == 0)
    def _():
        xb = x_ref[...].astype(BF16)
        xb_ref[...] = xb
        o_ref[...] = jnp.dot(xb, w_ref[...], preferred_element_type=F32).astype(BF16)
        lr_ref[...] = jnp.dot(xb, wlr_ref[...], preferred_element_type=F32)

    @pl.when(j > 0)
    def _():
        o_ref[...] = jnp.dot(xb_ref[...], w_ref[...],
                             preferred_element_type=F32).astype(BF16)


def _proj(x2, w_main, w_lr, tm=1024, tn=2560):
    m = x2.shape[0]
    return pl.pallas_call(
        _proj_kernel,
        grid=(m // tm, N_PROJ // tn),
        in_specs=[
            pl.BlockSpec((tm, D_MODEL), lambda i, j: (i, 0)),
            pl.BlockSpec((D_MODEL, tn), lambda i, j: (0, j)),
            pl.BlockSpec((D_MODEL, LANE), lambda i, j: (0, 0)),
        ],
        out_specs=[
            pl.BlockSpec((tm, tn), lambda i, j: (i, j)),
            pl.BlockSpec((tm, LANE), lambda i, j: (i, 0)),
        ],
        out_shape=[
            jax.ShapeDtypeStruct((m, N_PROJ), BF16),
            jax.ShapeDtypeStruct((m, LANE), F32),
        ],
        scratch_shapes=[pltpu.VMEM((tm, D_MODEL), BF16)],
        compiler_params=pltpu.CompilerParams(
            dimension_semantics=("arbitrary", "arbitrary"),
            vmem_limit_bytes=VMEM_LIMIT),
        name="proj",
    )(x2, w_main, w_lr)


def _gla_kernel(q_ref, k_ref, v_ref, go_ref, lr_ref, wup_ref, ba_ref, nw_ref,
                o_ref, st_ref, *, t_rows, n_parts):
    c = GLA_CHUNK
    cs = 2 * LANE

    @pl.when(pl.program_id(2) == 0)
    def _():
        st_ref[...] = jnp.zeros_like(st_ref)

    row = lax.broadcasted_iota(jnp.int32, (cs, cs), 0)
    col = lax.broadcasted_iota(jnp.int32, (cs, cs), 1)
    tri = jnp.where((row >= col) & (row // c == col // c), 1.0, 0.0).astype(BF16)

    half = t_rows // n_parts
    hc = half // c

    def stage_z(s):
        return jnp.dot(lr_ref[s:s + half, :].astype(BF16), wup_ref[...],
                       preferred_element_type=F32) + ba_ref[...]

    def stage_cumsum(z):
        softplus2 = jnp.log2(1.0 + jnp.exp2(jnp.abs(z) * (-LOG2E)))
        la = (jnp.minimum(z, 0.0) * LOG2E - softplus2) * (1.0 / GLA_TAU)
        la_hi = la.astype(BF16)
        la_lo = (la - la_hi.astype(F32)).astype(BF16)
        return jnp.concatenate(
            [jnp.dot(tri, la_hi[r:r + cs], preferred_element_type=F32)
             + jnp.dot(tri, la_lo[r:r + cs], preferred_element_type=F32)
             for r in range(0, half, cs)], axis=0)

    def stage_qk(s, b):
        sub = [slice(i * c, (i + 1) * c) for i in range(hc)]
        b_last = [b[r.stop - 1:r.stop, :] for r in sub]
        bl = jnp.concatenate([jnp.broadcast_to(r, (c, GLA_DK)) for r in b_last], axis=0)
        q = q_ref[s:s + half, :]
        k = k_ref[s:s + half, :]
        q_in = q * jnp.exp2(b).astype(BF16)
        k_in = k * jnp.exp2(-b).astype(BF16)
        k_dec = k * jnp.exp2(bl - b).astype(BF16)

        def span(lo, hi):
            return functools.reduce(lambda a, b_: a + b_, b_last[lo:hi])

        def scaled(x, lo, hi):
            return x if lo >= hi else x * jnp.exp2(span(lo, hi)).astype(BF16)

        q_part = jnp.concatenate([scaled(q_in[r], 0, i) for i, r in enumerate(sub)], axis=0)
        k_part = jnp.concatenate([scaled(k_dec[r], i + 1, hc) for i, r in enumerate(sub)],
                                 axis=0)
        k_seen = [jnp.concatenate([scaled(k_dec[sub[j]], j + 1, i) for j in range(i)]
                                  + [k_in[r]], axis=0)
                  for i, r in enumerate(sub)]
        return q_in, q_part, k_part, k_seen, jnp.exp2(span(0, hc))

    def stage_mm(s, q_in, k_part, k_seen):
        scores = []
        for i in range(hc):
            n_keys = (i + 1) * c
            sc = lax.dot_general(q_in[i * c:(i + 1) * c], k_seen[i], (((1,), (1,)), ((), ())),
                                 preferred_element_type=F32)
            qi = lax.broadcasted_iota(jnp.int32, (c, n_keys), 0)
            kj = lax.broadcasted_iota(jnp.int32, (c, n_keys), 1)
            scores.append(jnp.where(kj - i * c <= qi, sc, 0.0).astype(BF16))
        upd = lax.dot_general(v_ref[s:s + half, :], k_part, (((0,), (0,)), ((), ())),
                              preferred_element_type=F32)
        return scores, upd

    def stage_recur(s, st, q_part, scores, upd, decay):
        intra = jnp.concatenate(
            [jnp.dot(scores[i], v_ref[s:s + (i + 1) * c, :], preferred_element_type=F32)
             for i in range(hc)], axis=0)
        inter = lax.dot_general(q_part, st.astype(BF16), (((1,), (1,)), ((), ())),
                                preferred_element_type=F32)
        return st * decay + upd, intra + inter

    def stage_out(s, o):
        rms = lax.rsqrt(jnp.mean(o * o, axis=-1, keepdims=True) + RMS_EPS * GLA_DK)
        g = go_ref[s:s + half, :]
        silu = g * (0.5 + 0.5 * jnp.tanh(0.5 * g))
        o_ref[s:s + half, :] = (o.astype(BF16) * rms.astype(BF16)
                                * nw_ref[...].astype(BF16) * silu)

    starts = list(range(0, t_rows, half))
    st = st_ref[...]
    qk = stage_qk(starts[0], stage_cumsum(stage_z(starts[0])))
    for p, s in enumerate(starts):
        nxt = starts[p + 1] if p + 1 < len(starts) else None
        q_in, q_part, k_part, k_seen, decay = qk
        if nxt is not None:
            z_n = stage_z(nxt)
        scores, upd = stage_mm(s, q_in, k_part, k_seen)
        if nxt is not None:
            b_n = stage_cumsum(z_n)
        st, o = stage_recur(s, st, q_part, scores, upd, decay)
        if nxt is not None:
            qk = stage_qk(nxt, b_n)
        stage_out(s, o)
    st_ref[...] = st


def _gla(proj, glr, wup_pad, b_alpha2, norm_w2, batch, seq, t_rows=2048, n_parts=8):
    m = proj.shape[0]
    nt = seq // t_rows
    row = lambda b, h, t: b * nt + t
    return pl.pallas_call(
        functools.partial(_gla_kernel, t_rows=t_rows, n_parts=n_parts),
        grid=(batch, GLA_HEADS, nt),
        in_specs=[
            pl.BlockSpec((t_rows, GLA_DK), lambda b, h, t: (row(b, h, t), Q_OFF // GLA_DK + h)),
            pl.BlockSpec((t_rows, GLA_DK), lambda b, h, t: (row(b, h, t), K_OFF // GLA_DK + h)),
            pl.BlockSpec((t_rows, GLA_DV), lambda b, h, t: (row(b, h, t), V_OFF // GLA_DV + h)),
            pl.BlockSpec((t_rows, GLA_DV), lambda b, h, t: (row(b, h, t), GO_OFF // GLA_DV + h)),
            pl.BlockSpec((t_rows, LANE), lambda b, h, t: (row(b, h, t), 0)),
            pl.BlockSpec((LANE, GLA_DK), lambda b, h, t: (0, h)),
            pl.BlockSpec((1, GLA_DK), lambda b, h, t: (0, h)),
            pl.BlockSpec((1, GLA_DV), lambda b, h, t: (0, 0)),
        ],
        out_specs=pl.BlockSpec((t_rows, GLA_DV), lambda b, h, t: (row(b, h, t), h)),
        out_shape=jax.ShapeDtypeStruct((m, GLA_HEADS * GLA_DV), BF16),
        scratch_shapes=[pltpu.VMEM((GLA_DV, GLA_DK), F32)],
        compiler_params=pltpu.CompilerParams(
            dimension_semantics=("arbitrary", "arbitrary", "arbitrary"),
            vmem_limit_bytes=VMEM_LIMIT),
        name="gla",
    )(proj, proj, proj, proj, glr, wup_pad, b_alpha2, norm_w2)


def _alibi_slopes():
    h = np.arange(1, SWA_HEADS + 1, dtype=np.float32)
    return [float(s) for s in (2.0 ** (-8.0 * h / SWA_HEADS)).astype(np.float32)]


MASKED = -1e30
ONES_ROWS = 16
SWA_LOOKAHEAD = 5


def _swa_fill_tables(slope_ref, tbl_ref):
    w = WINDOW
    s = lax.broadcasted_iota(jnp.int32, (2 * w, w), 0)
    t = lax.broadcasted_iota(jnp.int32, (2 * w, w), 1)
    dist = t - s + w
    band = (dist >= 0) & (dist < w)
    ndist = -dist.astype(F32)

    def fill(h, carry):
        bias = (slope_ref[h] * LOG2E) * ndist
        tbl_ref[0, h] = jnp.where(band & (s >= w), bias, MASKED)
        tbl_ref[1, h] = jnp.where(band, bias, MASKED)
        return carry

    lax.fori_loop(0, SWA_HEADS, fill, 0)


def _swa_blocks(n, sink_ref, q_ref, kp_ref, kc_ref, vp_ref, vc_ref, o_ref, tbl_ref,
                q_blocks, between=None):
    w = WINDOW
    lane = lax.broadcasted_iota(jnp.int32, (2 * w, LANE), 1)
    low = lane < SWA_HD
    lane_q = lax.broadcasted_iota(jnp.int32, (w, LANE), 1)
    low_q = lane_q < SWA_HD
    ones = jnp.ones((ONES_ROWS, 2 * w), BF16)

    kall = jnp.concatenate([kp_ref[...], kc_ref[...]], axis=0)
    vall = jnp.concatenate([vp_ref[...], vc_ref[...]], axis=0)
    n_pairs = SWA_HEADS // 2
    pairs_per_kv = SWA_GROUP // 2

    def window(blk):
        kfull = kall[blk * w:(blk + 2) * w, :]
        v_t = vall[blk * w:(blk + 2) * w, :].astype(F32).T.astype(BF16)
        kdups = []
        for kvp in range(SWA_KV_HEADS // 2):
            kcol = kfull[:, kvp * LANE:(kvp + 1) * LANE].astype(F32) * LOG2E
            krol = pltpu.roll(kcol, SWA_HD, 1)
            kdups.append(jnp.where(low, kcol, krol).astype(BF16))
            kdups.append(jnp.where(low, krol, kcol).astype(BF16))
        vaugs = [jnp.concatenate([v_t[kv * SWA_HD:(kv + 1) * SWA_HD, :], ones], axis=0)
                 for kv in range(SWA_KV_HEADS)]
        return kdups, vaugs

    windows = [window(blk) for blk in range(q_blocks)]

    def logits_t(blk, p_idx):
        qp = (q_ref[blk * w:(blk + 1) * w, p_idx * LANE:(p_idx + 1) * LANE]
              * (SWA_HD ** -0.5))
        qw = jnp.concatenate([jnp.where(low_q, qp, 0.0),
                              jnp.where(low_q, 0.0, qp)], axis=0)
        return lax.dot_general(windows[blk][0][p_idx // pairs_per_kv], qw,
                               (((1,), (1,)), ((), ())),
                               preferred_element_type=F32)

    def finish(blk, p_idx, lg):
        var = jnp.minimum(n, 1) if blk == 0 else 1
        ps, ms = [], []
        for e in range(2):
            h = 2 * p_idx + e
            l = lg[:, e * w:(e + 1) * w] + tbl_ref[var, h]
            mx = jnp.maximum(jnp.max(l, axis=0, keepdims=True), sink_ref[h] * LOG2E)
            ps.append(jnp.exp2(l - mx).astype(BF16))
            ms.append(mx)
        ot = jnp.dot(windows[blk][1][p_idx // pairs_per_kv], jnp.concatenate(ps, axis=1),
                     preferred_element_type=F32)
        halves = []
        for e in range(2):
            h = 2 * p_idx + e
            den = (ot[SWA_HD:SWA_HD + 1, e * w:(e + 1) * w]
                   + jnp.exp2(sink_ref[h] * LOG2E - ms[e]))
            halves.append(ot[0:SWA_HD, e * w:(e + 1) * w] * (1.0 / den))
        o_ref[blk * w:(blk + 1) * w, p_idx * LANE:(p_idx + 1) * LANE] = (
            jnp.concatenate(halves, axis=0).T.astype(BF16))

    units = [(blk, p) for blk in range(q_blocks) for p in range(n_pairs)]
    pending = {u: logits_t(*u) for u in units[:SWA_LOOKAHEAD]}
    for idx, u in enumerate(units):
        finish(*u, pending.pop(u))
        if idx + SWA_LOOKAHEAD < len(units):
            nu = units[idx + SWA_LOOKAHEAD]
            pending[nu] = logits_t(*nu)
        if between is not None:
            between(idx)


def _attn_merge_kernel(sink_ref, slope_ref, q_ref, kp_ref, kc_ref, vp_ref, vc_ref,
                       oa_ref, gg_ref, gs_ref, x_ref, wg_ref, ws_ref, wo_ref,
                       g_ref, b_ref, o_ref, tbl_ref, ob_ref, *, q_blocks):
    n = pl.program_id(1)

    @pl.when((pl.program_id(0) == 0) & (n == 0))
    def _():
        _swa_fill_tables(slope_ref, tbl_ref)

    n_units = q_blocks * (SWA_HEADS // 2)
    n_chunks = D_MODEL // MXU_COLS
    every = n_units // n_chunks
    ya_chunks = []

    def between(idx):
        if (idx + 1) % every == 0:
            c0 = len(ya_chunks) * MXU_COLS
            ya_chunks.append(jnp.dot(oa_ref[...], wg_ref[:, c0:c0 + MXU_COLS],
                                     preferred_element_type=F32))

    _swa_blocks(n, sink_ref, q_ref, kp_ref, kc_ref, vp_ref, vc_ref, ob_ref, tbl_ref,
                q_blocks, between)
    ya = jnp.concatenate(ya_chunks, axis=1)
    yb = jnp.dot(ob_ref[...], ws_ref[...], preferred_element_type=F32)
    merged = (jax.nn.sigmoid(gg_ref[...].astype(F32)) * ya
              + jax.nn.sigmoid(gs_ref[...].astype(F32)) * yb)
    mix = jnp.dot(merged.astype(BF16), wo_ref[...], preferred_element_type=F32)
    o_ref[...] = _layer_norm(ALPHA * x_ref[...] + mix, g_ref[...], b_ref[...])


def _attn_merge(proj, sinks, o_a, x2, wg, ws, wo, ln_g, ln_b, batch, seq, q_blocks=2):
    m = proj.shape[0]
    w = WINDOW
    tq = q_blocks * w
    ns = seq // tq
    kvw = SWA_KV_HEADS * SWA_HD
    cur = lambda b, n: b * ns + n
    prev = lambda b, n: (b * ns + n) * q_blocks - jnp.minimum(n, 1)
    rows = lambda col: pl.BlockSpec((tq, D_MODEL), lambda b, n: (cur(b, n), col))
    wspec = pl.BlockSpec((D_MODEL, D_MODEL), lambda b, n: (0, 0),
                         pipeline_mode=pl.Buffered(1))
    vspec = pl.BlockSpec((1, D_MODEL), lambda b, n: (0, 0))
    return pl.pallas_call(
        functools.partial(_attn_merge_kernel, q_blocks=q_blocks),
        grid=(batch, ns),
        in_specs=[
            pl.BlockSpec(memory_space=pltpu.SMEM),
            pl.BlockSpec(memory_space=pltpu.SMEM),
            rows(SQ_OFF // D_MODEL),
            pl.BlockSpec((w, kvw), lambda b, n: (prev(b, n), SK_OFF // kvw)),
            pl.BlockSpec((tq, kvw), lambda b, n: (cur(b, n), SK_OFF // kvw)),
            pl.BlockSpec((w, kvw), lambda b, n: (prev(b, n), SV_OFF // kvw)),
            pl.BlockSpec((tq, kvw), lambda b, n: (cur(b, n), SV_OFF // kvw)),
            rows(0), rows(GG_OFF // D_MODEL), rows(GS_OFF // D_MODEL), rows(0),
            wspec, wspec, wspec, vspec, vspec,
        ],
        out_specs=rows(0),
        out_shape=jax.ShapeDtypeStruct((m, D_MODEL), F32),
        scratch_shapes=[pltpu.VMEM((2, SWA_HEADS, 2 * w, w), F32),
                        pltpu.VMEM((tq, D_MODEL), BF16)],
        compiler_params=pltpu.CompilerParams(
            dimension_semantics=("arbitrary", "arbitrary"),
            vmem_limit_bytes=FFN_VMEM_LIMIT),
        name="attn_merge",
    )(sinks, jnp.asarray(_alibi_slopes(), F32), proj, proj, proj, proj, proj,
      o_a, proj, proj, x2, wg, ws, wo, ln_g, ln_b)


def _ffn_kernel(x_ref, wu_ref, wd_ref, g_ref, b_ref, o_ref, xb_ref):
    j = pl.program_id(1)

    @pl.when(j == 0)
    def _():
        xb_ref[...] = x_ref[...].astype(BF16)
        o_ref[...] = jnp.zeros_like(o_ref)

    h = jnp.dot(xb_ref[...], wu_ref[...], preferred_element_type=F32)
    h = jnp.maximum(h, 0.0)
    h = (h * h).astype(BF16)
    o_ref[...] += jnp.dot(h, wd_ref[...], preferred_element_type=F32)

    @pl.when(j == pl.num_programs(1) - 1)
    def _():
        o_ref[...] = _layer_norm(ALPHA * x_ref[...] + o_ref[...],
                                 g_ref[...], b_ref[...])


def _ffn(x1, wu, wd, ln_g, ln_b, tm=512, tf=2048):
    m = x1.shape[0]
    vspec = pl.BlockSpec((1, D_MODEL), lambda i, j: (0, 0))
    return pl.pallas_call(
        _ffn_kernel,
        grid=(m // tm, D_FF // tf),
        in_specs=[
            pl.BlockSpec((tm, D_MODEL), lambda i, j: (i, 0)),
            pl.BlockSpec((D_MODEL, tf), lambda i, j: (0, j)),
            pl.BlockSpec((tf, D_MODEL), lambda i, j: (j, 0)),
            vspec, vspec,
        ],
        out_specs=pl.BlockSpec((tm, D_MODEL), lambda i, j: (i, 0)),
        out_shape=jax.ShapeDtypeStruct((m, D_MODEL), F32),
        scratch_shapes=[pltpu.VMEM((tm, D_MODEL), BF16)],
        compiler_params=pltpu.CompilerParams(
            dimension_semantics=("arbitrary", "arbitrary"),
            vmem_limit_bytes=FFN_VMEM_LIMIT),
        name="ffn",
    )(x1, wu, wd, ln_g, ln_b)


def kernel(x, w_in, w_alpha_up, b_alpha, gla_norm_w, attn_sinks, w_branch_gla,
           w_branch_swa, w_out, ln1_g, ln1_b, w_ff_up, w_ff_down, ln2_g, ln2_b):
    batch, seq, d = x.shape
    x2 = x.reshape(batch * seq, d)

    w_tail = w_in[:, 6160:12816].astype(BF16)
    w_main = jnp.concatenate(
        [w_in[:, 0:6144].astype(BF16), w_tail[:, 0:2048], w_tail[:, 2560:4608],
         w_tail[:, 4608:6656], w_tail[:, 2048:2560]], axis=1)
    w_lr = jnp.pad(w_in[:, 6144:6160], ((0, 0), (0, LANE - GLA_RANK))).astype(BF16)
    wup_pad = jnp.pad(w_alpha_up, ((0, LANE - GLA_RANK), (0, 0))).astype(BF16)

    proj, glr = _proj(x2, w_main, w_lr)
    o_a = _gla(proj, glr, wup_pad, b_alpha.reshape(1, -1).astype(F32),
               gla_norm_w.reshape(1, -1).astype(F32), batch, seq)
    x1 = _attn_merge(proj, attn_sinks.astype(F32), o_a, x2, w_branch_gla.astype(BF16),
                     w_branch_swa.astype(BF16), w_out.astype(BF16),
                     ln1_g.reshape(1, -1), ln1_b.reshape(1, -1), batch, seq)
    out = _ffn(x1, w_ff_up.astype(BF16), w_ff_down.astype(BF16),
               ln2_g.reshape(1, -1), ln2_b.reshape(1, -1))
    return out.reshape(batch, seq, d)
```

```python
import functools

import jax
import jax.numpy as jnp
import numpy as np
from jax import lax
from jax.experimental import pallas as pl
from jax.experimental.pallas import tpu as pltpu

F32 = jnp.float32
BF16 = jnp.bfloat16

D_MODEL = 2048
GLA_HEADS = 4
GLA_DK = 256
GLA_DV = 512
GLA_RANK = 16
GLA_TAU = 16.0
GLA_CHUNK = 64
SWA_HEADS = 32
SWA_KV_HEADS = 4
SWA_HD = 64
SWA_GROUP = 8
WINDOW = 128
D_FF = 4 * D_MODEL
ALPHA = 2.0 ** 0.25
LN_EPS = 1e-5
RMS_EPS = 1e-6
LOG2E = 1.4426950408889634

Q_OFF, K_OFF, V_OFF, GO_OFF, SQ_OFF, GG_OFF, GS_OFF, SK_OFF, SV_OFF = (
    0, 1024, 2048, 4096, 6144, 8192, 10240, 12288, 12544)
N_PROJ = 12800
LANE = 128
MXU_COLS = 256

VMEM_LIMIT = 56 * 1024 * 1024
FFN_VMEM_LIMIT = 60 * 1024 * 1024


def _layer_norm(z, g, b):
    mu = jnp.mean(z, axis=-1, keepdims=True)
    zc = z - mu
    var = jnp.mean(zc * zc, axis=-1, keepdims=True)
    return zc * lax.rsqrt(var + LN_EPS) * g + b


def _proj_kernel(x_ref, w_ref, wlr_ref, o_ref, lr_ref, xb_ref):
    j = pl.program_id(1)

    @pl.when(j == 0)
    def _():
        xb = x_ref[...].astype(BF16)
        xb_ref[...] = xb
        o_ref[...] = jnp.dot(xb, w_ref[...], preferred_element_type=F32).astype(BF16)
        lr_ref[...] = jnp.dot(xb, wlr_ref[...], preferred_element_type=F32)

    @pl.when(j > 0)
    def _():
        o_ref[...] = jnp.dot(xb_ref[...], w_ref[...],
                             preferred_element_type=F32).astype(BF16)


def _proj(x2, w_main, w_lr, tm=1024, tn=2560):
    m = x2.shape[0]
    return pl.pallas_call(
        _proj_kernel,
        grid=(m // tm, N_PROJ // tn),
        in_specs=[
            pl.BlockSpec((tm, D_MODEL), lambda i, j: (i, 0)),
            pl.BlockSpec((D_MODEL, tn), lambda i, j: (0, j)),
            pl.BlockSpec((D_MODEL, LANE), lambda i, j: (0, 0)),
        ],
        out_specs=[
            pl.BlockSpec((tm, tn), lambda i, j: (i, j)),
            pl.BlockSpec((tm, LANE), lambda i, j: (i, 0)),
        ],
        out_shape=[
            jax.ShapeDtypeStruct((m, N_PROJ), BF16),
            jax.ShapeDtypeStruct((m, LANE), F32),
        ],
        scratch_shapes=[pltpu.VMEM((tm, D_MODEL), BF16)],
        compiler_params=pltpu.CompilerParams(
            dimension_semantics=("arbitrary", "arbitrary"),
            vmem_limit_bytes=VMEM_LIMIT),
        name="proj",
    )(x2, w_main, w_lr)


def _gla_kernel(q_ref, k_ref, v_ref, go_ref, lr_ref, wup_ref, ba_ref, nw_ref,
                o_ref, st_ref, *, t_rows, n_parts):
    c = GLA_CHUNK
    cs = 2 * LANE

    @pl.when(pl.program_id(2) == 0)
    def _():
        st_ref[...] = jnp.zeros_like(st_ref)

    row = lax.broadcasted_iota(jnp.int32, (cs, cs), 0)
    col = lax.broadcasted_iota(jnp.int32, (cs, cs), 1)
    tri = jnp.where((row >= col) & (row // c == col // c), 1.0, 0.0).astype(BF16)

    half = t_rows // n_parts
    hc = half // c

    def stage_z(s):
        return jnp.dot(lr_ref[s:s + half, :].astype(BF16), wup_ref[...],
                       preferred_element_type=F32) + ba_ref[...]

    def stage_cumsum(z):
        softplus2 = jnp.log2(1.0 + jnp.exp2(jnp.abs(z) * (-LOG2E)))
        la = (jnp.minimum(z, 0.0) * LOG2E - softplus2) * (1.0 / GLA_TAU)
        la_hi = la.astype(BF16)
        la_lo = (la - la_hi.astype(F32)).astype(BF16)
        return jnp.concatenate(
            [jnp.dot(tri, la_hi[r:r + cs], preferred_element_type=F32)
             + jnp.dot(tri, la_lo[r:r + cs], preferred_element_type=F32)
             for r in range(0, half, cs)], axis=0)

    def stage_qk(s, b):
        sub = [slice(i * c, (i + 1) * c) for i in range(hc)]
        b_last = [b[r.stop - 1:r.stop, :] for r in sub]
        bl = jnp.concatenate([jnp.broadcast_to(r, (c, GLA_DK)) for r in b_last], axis=0)
        q = q_ref[s:s + half, :]
        k = k_ref[s:s + half, :]
        q_in = q * jnp.exp2(b).astype(BF16)
        k_in = k * jnp.exp2(-b).astype(BF16)
        k_dec = k * jnp.exp2(bl - b).astype(BF16)

        def span(lo, hi):
            return functools.reduce(lambda a, b_: a + b_, b_last[lo:hi])

        def scaled(x, lo, hi):
            return x if lo >= hi else x * jnp.exp2(span(lo, hi)).astype(BF16)

        q_part = jnp.concatenate([scaled(q_in[r], 0, i) for i, r in enumerate(sub)], axis=0)
        k_part = jnp.concatenate([scaled(k_dec[r], i + 1, hc) for i, r in enumerate(sub)],
                                 axis=0)
        k_seen = [jnp.concatenate([scaled(k_dec[sub[j]], j + 1, i) for j in range(i)]
                                  + [k_in[r]], axis=0)
                  for i, r in enumerate(sub)]
        return q_in, q_part, k_part, k_seen, jnp.exp2(span(0, hc))

    def stage_mm(s, q_in, k_part, k_seen):
        scores = []
        for i in range(hc):
            n_keys = (i + 1) * c
            sc = lax.dot_general(q_in[i * c:(i + 1) * c], k_seen[i], (((1,), (1,)), ((), ())),
                                 preferred_element_type=F32)
            qi = lax.broadcasted_iota(jnp.int32, (c, n_keys), 0)
            kj = lax.broadcasted_iota(jnp.int32, (c, n_keys), 1)
            scores.append(jnp.where(kj - i * c <= qi, sc, 0.0).astype(BF16))
        upd = lax.dot_general(v_ref[s:s + half, :], k_part, (((0,), (0,)), ((), ())),
                              preferred_element_type=F32)
        return scores, upd

    def stage_recur(s, st, q_part, scores, upd, decay):
        intra = jnp.concatenate(
            [jnp.dot(scores[i], v_ref[s:s + (i + 1) * c, :], preferred_element_type=F32)
             for i in range(hc)], axis=0)
        inter = lax.dot_general(q_part, st.astype(BF16), (((1,), (1,)), ((), ())),
                                preferred_element_type=F32)
        return st * decay + upd, intra + inter

    def stage_out(s, o):
        rms = lax.rsqrt(jnp.mean(o * o, axis=-1, keepdims=True) + RMS_EPS * GLA_DK)
        g = go_ref[s:s + half, :]
        silu = g * (0.5 + 0.5 * jnp.tanh(0.5 * g))
        o_ref[s:s + half, :] = (o.astype(BF16) * rms.astype(BF16)
                                * nw_ref[...].astype(BF16) * silu)

    starts = list(range(0, t_rows, half))
    st = st_ref[...]
    qk = stage_qk(starts[0], stage_cumsum(stage_z(starts[0])))
    for p, s in enumerate(starts):
        nxt = starts[p + 1] if p + 1 < len(starts) else None
        q_in, q_part, k_part, k_seen, decay = qk
        if nxt is not None:
            z_n = stage_z(nxt)
        scores, upd = stage_mm(s, q_in, k_part, k_seen)
        if nxt is not None:
            b_n = stage_cumsum(z_n)
        st, o = stage_recur(s, st, q_part, scores, upd, decay)
        if nxt is not None:
            qk = stage_qk(nxt, b_n)
        stage_out(s, o)
    st_ref[...] = st


def _gla(proj, glr, wup_pad, b_alpha2, norm_w2, batch, seq, t_rows=2048, n_parts=8):
    m = proj.shape[0]
    nt = seq // t_rows
    row = lambda b, h, t: b * nt + t
    return pl.pallas_call(
        functools.partial(_gla_kernel, t_rows=t_rows, n_parts=n_parts),
        grid=(batch, GLA_HEADS, nt),
        in_specs=[
            pl.BlockSpec((t_rows, GLA_DK), lambda b, h, t: (row(b, h, t), Q_OFF // GLA_DK + h)),
            pl.BlockSpec((t_rows, GLA_DK), lambda b, h, t: (row(b, h, t), K_OFF // GLA_DK + h)),
            pl.BlockSpec((t_rows, GLA_DV), lambda b, h, t: (row(b, h, t), V_OFF // GLA_DV + h)),
            pl.BlockSpec((t_rows, GLA_DV), lambda b, h, t: (row(b, h, t), GO_OFF // GLA_DV + h)),
            pl.BlockSpec((t_rows, LANE), lambda b, h, t: (row(b, h, t), 0)),
            pl.BlockSpec((LANE, GLA_DK), lambda b, h, t: (0, h)),
            pl.BlockSpec((1, GLA_DK), lambda b, h, t: (0, h)),
            pl.BlockSpec((1, GLA_DV), lambda b, h, t: (0, 0)),
        ],
        out_specs=pl.BlockSpec((t_rows, GLA_DV), lambda b, h, t: (row(b, h, t), h)),
        out_shape=jax.ShapeDtypeStruct((m, GLA_HEADS * GLA_DV), BF16),
        scratch_shapes=[pltpu.VMEM((GLA_DV, GLA_DK), F32)],
        compiler_params=pltpu.CompilerParams(
            dimension_semantics=("arbitrary", "arbitrary", "arbitrary"),
            vmem_limit_bytes=VMEM_LIMIT),
        name="gla",
    )(proj, proj, proj, proj, glr, wup_pad, b_alpha2, norm_w2)


def _alibi_slopes():
    h = np.arange(1, SWA_HEADS + 1, dtype=np.float32)
    return [float(s) for s in (2.0 ** (-8.0 * h / SWA_HEADS)).astype(np.float32)]


MASKED = -1e30
ONES_ROWS = 16
SWA_LOOKAHEAD = 5


def _swa_fill_tables(slope_ref, tbl_ref):
    w = WINDOW
    s = lax.broadcasted_iota(jnp.int32, (2 * w, w), 0)
    t = lax.broadcasted_iota(jnp.int32, (2 * w, w), 1)
    dist = t - s + w
    band = (dist >= 0) & (dist < w)
    ndist = -dist.astype(F32)

    def fill(h, carry):
        bias = (slope_ref[h] * LOG2E) * ndist
        tbl_ref[0, h] = jnp.where(band & (s >= w), bias, MASKED)
        tbl_ref[1, h] = jnp.where(band, bias, MASKED)
        return carry

    lax.fori_loop(0, SWA_HEADS, fill, 0)


def _swa_blocks(n, sink_ref, q_ref, kp_ref, kc_ref, vp_ref, vc_ref, o_ref, tbl_ref,
                q_blocks, between=None):
    w = WINDOW
    lane = lax.broadcasted_iota(jnp.int32, (2 * w, LANE), 1)
    low = lane < SWA_HD
    lane_q = lax.broadcasted_iota(jnp.int32, (w, LANE), 1)
    low_q = lane_q < SWA_HD
    ones = jnp.ones((ONES_ROWS, 2 * w), BF16)

    kall = jnp.concatenate([kp_ref[...], kc_ref[...]], axis=0)
    vall = jnp.concatenate([vp_ref[...], vc_ref[...]], axis=0)
    n_pairs = SWA_HEADS // 2
    pairs_per_kv = SWA_GROUP // 2

    def window(blk):
        kfull = kall[blk * w:(blk + 2) * w, :]
        v_t = vall[blk * w:(blk + 2) * w, :].astype(F32).T.astype(BF16)
        kdups = []
        for kvp in range(SWA_KV_HEADS // 2):
            kcol = kfull[:, kvp * LANE:(kvp + 1) * LANE].astype(F32) * LOG2E
            krol = pltpu.roll(kcol, SWA_HD, 1)
            kdups.append(jnp.where(low, kcol, krol).astype(BF16))
            kdups.append(jnp.where(low, krol, kcol).astype(BF16))
        vaugs = [jnp.concatenate([v_t[kv * SWA_HD:(kv + 1) * SWA_HD, :], ones], axis=0)
                 for kv in range(SWA_KV_HEADS)]
        return kdups, vaugs

    windows = [window(blk) for blk in range(q_blocks)]

    def logits_t(blk, p_idx):
        qp = (q_ref[blk * w:(blk + 1) * w, p_idx * LANE:(p_idx + 1) * LANE]
              * (SWA_HD ** -0.5))
        qw = jnp.concatenate([jnp.where(low_q, qp, 0.0),
                              jnp.where(low_q, 0.0, qp)], axis=0)
        return lax.dot_general(windows[blk][0][p_idx // pairs_per_kv], qw,
                               (((1,), (1,)), ((), ())),
                               preferred_element_type=F32)

    def finish(blk, p_idx, lg):
        var = jnp.minimum(n, 1) if blk == 0 else 1
        ps, ms = [], []
        for e in range(2):
            h = 2 * p_idx + e
            l = lg[:, e * w:(e + 1) * w] + tbl_ref[var, h]
            mx = jnp.maximum(jnp.max(l, axis=0, keepdims=True), sink_ref[h] * LOG2E)
            ps.append(jnp.exp2(l - mx).astype(BF16))
            ms.append(mx)
        ot = jnp.dot(windows[blk][1][p_idx // pairs_per_kv], jnp.concatenate(ps, axis=1),
                     preferred_element_type=F32)
        halves = []
        for e in range(2):
            h = 2 * p_idx + e
            den = (ot[SWA_HD:SWA_HD + 1, e * w:(e + 1) * w]
                   + jnp.exp2(sink_ref[h] * LOG2E - ms[e]))
            halves.append(ot[0:SWA_HD, e * w:(e + 1) * w] * (1.0 / den))
        o_ref[blk * w:(blk + 1) * w, p_idx * LANE:(p_idx + 1) * LANE] = (
            jnp.concatenate(halves, axis=0).T.astype(BF16))

    units = [(blk, p) for blk in range(q_blocks) for p in range(n_pairs)]
    pending = {u: logits_t(*u) for u in units[:SWA_LOOKAHEAD]}
    for idx, u in enumerate(units):
        finish(*u, pending.pop(u))
        if idx + SWA_LOOKAHEAD < len(units):
            nu = units[idx + SWA_LOOKAHEAD]
            pending[nu] = logits_t(*nu)
        if between is not None:
            between(idx)


def _attn_merge_kernel(sink_ref, slope_ref, q_ref, kp_ref, kc_ref, vp_ref, vc_ref,
                       oa_ref, gg_ref, gs_ref, x_ref, wg_ref, ws_ref, wo_ref,
                       g_ref, b_ref, o_ref, tbl_ref, ob_ref, *, q_blocks):
    n = pl.program_id(1)

    @pl.when((pl.program_id(0) == 0) & (n == 0))
    def _():
        _swa_fill_tables(slope_ref, tbl_ref)

    n_units = q_blocks * (SWA_HEADS // 2)
    n_chunks = D_MODEL // MXU_COLS
    every = n_units // n_chunks
    ya_chunks = []

    def between(idx):
        if (idx + 1) % every == 0:
            c0 = len(ya_chunks) * MXU_COLS
            ya_chunks.append(jnp.dot(oa_ref[...], wg_ref[:, c0:c0 + MXU_COLS],
                                     preferred_element_type=F32))

    _swa_blocks(n, sink_ref, q_ref, kp_ref, kc_ref, vp_ref, vc_ref, ob_ref, tbl_ref,
                q_blocks, between)
    ya = jnp.concatenate(ya_chunks, axis=1)
    yb = jnp.dot(ob_ref[...], ws_ref[...], preferred_element_type=F32)
    merged = (jax.nn.sigmoid(gg_ref[...].astype(F32)) * ya
              + jax.nn.sigmoid(gs_ref[...].astype(F32)) * yb)
    mix = jnp.dot(merged.astype(BF16), wo_ref[...], preferred_element_type=F32)
    o_ref[...] = _layer_norm(ALPHA * x_ref[...] + mix, g_ref[...], b_ref[...])


def _attn_merge(proj, sinks, o_a, x2, wg, ws, wo, ln_g, ln_b, batch, seq, q_blocks=2):
    m = proj.shape[0]
    w = WINDOW
    tq = q_blocks * w
    ns = seq // tq
    kvw = SWA_KV_HEADS * SWA_HD
    cur = lambda b, n: b * ns + n
    prev = lambda b, n: (b * ns + n) * q_blocks - jnp.minimum(n, 1)
    rows = lambda col: pl.BlockSpec((tq, D_MODEL), lambda b, n: (cur(b, n), col))
    wspec = pl.BlockSpec((D_MODEL, D_MODEL), lambda b, n: (0, 0),
                         pipeline_mode=pl.Buffered(1))
    vspec = pl.BlockSpec((1, D_MODEL), lambda b, n: (0, 0))
    return pl.pallas_call(
        functools.partial(_attn_merge_kernel, q_blocks=q_blocks),
        grid=(batch, ns),
        in_specs=[
            pl.BlockSpec(memory_space=pltpu.SMEM),
            pl.BlockSpec(memory_space=pltpu.SMEM),
            rows(SQ_OFF // D_MODEL),
            pl.BlockSpec((w, kvw), lambda b, n: (prev(b, n), SK_OFF // kvw)),
            pl.BlockSpec((tq, kvw), lambda b, n: (cur(b, n), SK_OFF // kvw)),
            pl.BlockSpec((w, kvw), lambda b, n: (prev(b, n), SV_OFF // kvw)),
            pl.BlockSpec((tq, kvw), lambda b, n: (cur(b, n), SV_OFF // kvw)),
            rows(0), rows(GG_OFF // D_MODEL), rows(GS_OFF // D_MODEL), rows(0),
            wspec, wspec, wspec, vspec, vspec,
        ],
        out_specs=rows(0),
        out_shape=jax.ShapeDtypeStruct((m, D_MODEL), F32),
        scratch_shapes=[pltpu.VMEM((2, SWA_HEADS, 2 * w, w), F32),
                        pltpu.VMEM((tq, D_MODEL), BF16)],
        compiler_params=pltpu.CompilerParams(
            dimension_semantics=("arbitrary", "arbitrary"),
            vmem_limit_bytes=FFN_VMEM_LIMIT),
        name="attn_merge",
    )(sinks, jnp.asarray(_alibi_slopes(), F32), proj, proj, proj, proj, proj,
      o_a, proj, proj, x2, wg, ws, wo, ln_g, ln_b)


def _ffn_kernel(x_ref, wu_ref, wd_ref, g_ref, b_ref, o_ref, xb_ref):
    j = pl.program_id(1)

    @pl.when(j == 0)
    def _():
        xb_ref[...] = x_ref[...].astype(BF16)
        o_ref[...] = jnp.zeros_like(o_ref)

    h = jnp.dot(xb_ref[...], wu_ref[...], preferred_element_type=F32)
    h = jnp.maximum(h, 0.0)
    h = (h * h).astype(BF16)
    o_ref[...] += jnp.dot(h, wd_ref[...], preferred_element_type=F32)

    @pl.when(j == pl.num_programs(1) - 1)
    def _():
        o_ref[...] = _layer_norm(ALPHA * x_ref[...] + o_ref[...],
                                 g_ref[...], b_ref[...])


def _ffn(x1, wu, wd, ln_g, ln_b, tm=512, tf=2048):
    m = x1.shape[0]
    vspec = pl.BlockSpec((1, D_MODEL), lambda i, j: (0, 0))
    return pl.pallas_call(
        _ffn_kernel,
        grid=(m // tm, D_FF // tf),
        in_specs=[
            pl.BlockSpec((tm, D_MODEL), lambda i, j: (i, 0)),
            pl.BlockSpec((D_MODEL, tf), lambda i, j: (0, j)),
            pl.BlockSpec((tf, D_MODEL), lambda i, j: (j, 0)),
            vspec, vspec,
        ],
        out_specs=pl.BlockSpec((tm, D_MODEL), lambda i, j: (i, 0)),
        out_shape=jax.ShapeDtypeStruct((m, D_MODEL), F32),
        scratch_shapes=[pltpu.VMEM((tm, D_MODEL), BF16)],
        compiler_params=pltpu.CompilerParams(
            dimension_semantics=("arbitrary", "arbitrary"),
            vmem_limit_bytes=FFN_VMEM_LIMIT),
        name="ffn",
    )(x1, wu, wd, ln_g, ln_b)


REPACK_COLS = 512
GLA_COLS = 6144


def _repack_src_block(t):
    first_gate = GG_OFF // REPACK_COLS
    first_kv = SK_OFF // REPACK_COLS
    return jnp.where(t < first_gate, t,
                     jnp.where(t < first_kv, t + 1, SQ_OFF // REPACK_COLS + D_MODEL // REPACK_COLS))


def _repack_kernel(a_ref, b_ref, o_ref):
    t = pl.program_id(0)
    n_aligned = GLA_COLS // REPACK_COLS

    @pl.when(t < n_aligned)
    def _():
        o_ref[...] = a_ref[...].astype(BF16)

    @pl.when(t >= n_aligned)
    def _():
        ar = pltpu.roll(a_ref[...], REPACK_COLS - GLA_RANK, 1)
        br = pltpu.roll(b_ref[...], LANE - GLA_RANK, 1)
        lane = lax.broadcasted_iota(jnp.int32, br.shape, 1)
        last = jnp.where(lane < LANE - GLA_RANK, ar[:, REPACK_COLS - LANE:], br)
        o_ref[...] = jnp.concatenate([ar[:, :REPACK_COLS - LANE], last], axis=1).astype(BF16)


def _repack_w_in(w_in):
    per = REPACK_COLS // LANE
    return pl.pallas_call(
        _repack_kernel,
        grid=(N_PROJ // REPACK_COLS,),
        in_specs=[
            pl.BlockSpec((D_MODEL, REPACK_COLS), lambda t: (0, _repack_src_block(t))),
            pl.BlockSpec((D_MODEL, LANE), lambda t: (0, (_repack_src_block(t) + 1) * per)),
        ],
        out_specs=pl.BlockSpec((D_MODEL, REPACK_COLS), lambda t: (0, t)),
        out_shape=jax.ShapeDtypeStruct((D_MODEL, N_PROJ), BF16),
        compiler_params=pltpu.CompilerParams(
            dimension_semantics=("arbitrary",),
            vmem_limit_bytes=VMEM_LIMIT),
        name="repack",
    )(w_in, w_in)


def kernel(x, w_in, w_alpha_up, b_alpha, gla_norm_w, attn_sinks, w_branch_gla,
           w_branch_swa, w_out, ln1_g, ln1_b, w_ff_up, w_ff_down, ln2_g, ln2_b):
    batch, seq, d = x.shape
    x2 = x.reshape(batch * seq, d)

    w_main = _repack_w_in(w_in)
    w_lr = jnp.pad(w_in[:, 6144:6160], ((0, 0), (0, LANE - GLA_RANK))).astype(BF16)
    wup_pad = jnp.pad(w_alpha_up, ((0, LANE - GLA_RANK), (0, 0))).astype(BF16)

    proj, glr = _proj(x2, w_main, w_lr)
    o_a = _gla(proj, glr, wup_pad, b_alpha.reshape(1, -1).astype(F32),
               gla_norm_w.reshape(1, -1).astype(F32), batch, seq)
    x1 = _attn_merge(proj, attn_sinks.astype(F32), o_a, x2, w_branch_gla.astype(BF16),
                     w_branch_swa.astype(BF16), w_out.astype(BF16),
                     ln1_g.reshape(1, -1), ln1_b.reshape(1, -1), batch, seq)
    out = _ffn(x1, w_ff_up.astype(BF16), w_ff_down.astype(BF16),
               ln2_g.reshape(1, -1), ln2_b.reshape(1, -1))
    return out.reshape(batch, seq, d)
```

```python
import functools

import jax
import jax.numpy as jnp
import numpy as np
from jax import lax
from jax.experimental import pallas as pl
from jax.experimental.pallas import tpu as pltpu

F32 = jnp.float32
BF16 = jnp.bfloat16

D_MODEL = 2048
GLA_HEADS = 4
GLA_DK = 256
GLA_DV = 512
GLA_RANK = 16
GLA_TAU = 16.0
GLA_CHUNK = 64
SWA_HEADS = 32
SWA_KV_HEADS = 4
SWA_HD = 64
SWA_GROUP = 8
WINDOW = 128
D_FF = 4 * D_MODEL
ALPHA = 2.0 ** 0.25
LN_EPS = 1e-5
RMS_EPS = 1e-6
LOG2E = 1.4426950408889634

Q_OFF, K_OFF, V_OFF, GO_OFF, SQ_OFF, GG_OFF, GS_OFF, SK_OFF, SV_OFF = (
    0, 1024, 2048, 4096, 6144, 8192, 10240, 12288, 12544)
N_PROJ = 12800
LANE = 128
MXU_COLS = 256

VMEM_LIMIT = 56 * 1024 * 1024
FFN_VMEM_LIMIT = 60 * 1024 * 1024


def _layer_norm(z, g, b):
    mu = jnp.mean(z, axis=-1, keepdims=True)
    zc = z - mu
    var = jnp.mean(zc * zc, axis=-1, keepdims=True)
    return zc * lax.rsqrt(var + LN_EPS) * g + b


def _proj_kernel(x_ref, w_ref, wlr_ref, o_ref, lr_ref, xb_ref):
    j = pl.program_id(1)

    @pl.when(j == 0)
    def _():
        xb = x_ref[...].astype(BF16)
        xb_ref[...] = xb
        o_ref[...] = jnp.dot(xb, w_ref[...], preferred_element_type=F32).astype(BF16)
        lr_ref[...] = jnp.dot(xb, wlr_ref[...], preferred_element_type=F32)

    @pl.when(j > 0)
    def _():
        o_ref[...] = jnp.dot(xb_ref[...], w_ref[...],
                             preferred_element_type=F32).astype(BF16)


def _proj(x2, w_main, w_lr, tm=1024, tn=2560):
    m = x2.shape[0]
    return pl.pallas_call(
        _proj_kernel,
        grid=(m // tm, N_PROJ // tn),
        in_specs=[
            pl.BlockSpec((tm, D_MODEL), lambda i, j: (i, 0)),
            pl.BlockSpec((D_MODEL, tn), lambda i, j: (0, j)),
            pl.BlockSpec((D_MODEL, LANE), lambda i, j: (0, 0)),
        ],
        out_specs=[
            pl.BlockSpec((tm, tn), lambda i, j: (i, j)),
            pl.BlockSpec((tm, LANE), lambda i, j: (i, 0)),
        ],
        out_shape=[
            jax.ShapeDtypeStruct((m, N_PROJ), BF16),
            jax.ShapeDtypeStruct((m, LANE), F32),
        ],
        scratch_shapes=[pltpu.VMEM((tm, D_MODEL), BF16)],
        compiler_params=pltpu.CompilerParams(
            dimension_semantics=("arbitrary", "arbitrary"),
            vmem_limit_bytes=VMEM_LIMIT),
        name="proj",
    )(x2, w_main, w_lr)


def _gla_kernel(q_ref, k_ref, v_ref, go_ref, lr_ref, wup_ref, ba_ref, nw_ref,
                o_ref, st_ref, *, t_rows, n_parts):
    c = GLA_CHUNK
    cs = 2 * LANE

    @pl.when(pl.program_id(2) == 0)
    def _():
        st_ref[...] = jnp.zeros_like(st_ref)

    row = lax.broadcasted_iota(jnp.int32, (cs, cs), 0)
    col = lax.broadcasted_iota(jnp.int32, (cs, cs), 1)
    tri = jnp.where((row >= col) & (row // c == col // c), 1.0, 0.0).astype(BF16)

    half = t_rows // n_parts
    hc = half // c

    def stage_z(s):
        return jnp.dot(lr_ref[s:s + half, :].astype(BF16), wup_ref[...],
                       preferred_element_type=F32) + ba_ref[...]

    def stage_cumsum(z):
        softplus2 = jnp.log2(1.0 + jnp.exp2(jnp.abs(z) * (-LOG2E)))
        la = (jnp.minimum(z, 0.0) * LOG2E - softplus2) * (1.0 / GLA_TAU)
        la_hi = la.astype(BF16)
        la_lo = (la - la_hi.astype(F32)).astype(BF16)
        return jnp.concatenate(
            [jnp.dot(tri, la_hi[r:r + cs], preferred_element_type=F32)
             + jnp.dot(tri, la_lo[r:r + cs], preferred_element_type=F32)
             for r in range(0, half, cs)], axis=0)

    def stage_qk(s, b):
        sub = [slice(i * c, (i + 1) * c) for i in range(hc)]
        b_last = [b[r.stop - 1:r.stop, :] for r in sub]
        bl = jnp.concatenate([jnp.broadcast_to(r, (c, GLA_DK)) for r in b_last], axis=0)
        q = q_ref[s:s + half, :]
        k = k_ref[s:s + half, :]
        q_in = q * jnp.exp2(b).astype(BF16)
        k_in = k * jnp.exp2(-b).astype(BF16)
        k_dec = k * jnp.exp2(bl - b).astype(BF16)

        def span(lo, hi):
            return functools.reduce(lambda a, b_: a + b_, b_last[lo:hi])

        def scaled(x, lo, hi):
            return x if lo >= hi else x * jnp.exp2(span(lo, hi)).astype(BF16)

        q_part = jnp.concatenate([scaled(q_in[r], 0, i) for i, r in enumerate(sub)], axis=0)
        k_part = jnp.concatenate([scaled(k_dec[r], i + 1, hc) for i, r in enumerate(sub)],
                                 axis=0)
        k_seen = [jnp.concatenate([scaled(k_dec[sub[j]], j + 1, i) for j in range(i)]
                                  + [k_in[r]], axis=0)
                  for i, r in enumerate(sub)]
        return q_in, q_part, k_part, k_seen, jnp.exp2(span(0, hc))

    def stage_mm(s, q_in, k_part, k_seen):
        scores = []
        for i in range(hc):
            n_keys = (i + 1) * c
            sc = lax.dot_general(q_in[i * c:(i + 1) * c], k_seen[i], (((1,), (1,)), ((), ())),
                                 preferred_element_type=F32)
            qi = lax.broadcasted_iota(jnp.int32, (c, n_keys), 0)
            kj = lax.broadcasted_iota(jnp.int32, (c, n_keys), 1)
            scores.append(jnp.where(kj - i * c <= qi, sc, 0.0).astype(BF16))
        upd = lax.dot_general(v_ref[s:s + half, :], k_part, (((0,), (0,)), ((), ())),
                              preferred_element_type=F32)
        return scores, upd

    def stage_recur(s, st, q_part, scores, upd, decay):
        intra = jnp.concatenate(
            [jnp.dot(scores[i], v_ref[s:s + (i + 1) * c, :], preferred_element_type=F32)
             for i in range(hc)], axis=0)
        inter = lax.dot_general(q_part, st.astype(BF16), (((1,), (1,)), ((), ())),
                                preferred_element_type=F32)
        return st * decay + upd, intra + inter

    def stage_out(s, o):
        rms = lax.rsqrt(jnp.mean(o * o, axis=-1, keepdims=True) + RMS_EPS * GLA_DK)
        g = go_ref[s:s + half, :]
        silu = g * (0.5 + 0.5 * jnp.tanh(0.5 * g))
        o_ref[s:s + half, :] = (o.astype(BF16) * rms.astype(BF16)
                                * nw_ref[...].astype(BF16) * silu)

    starts = list(range(0, t_rows, half))
    st = st_ref[...]
    qk = stage_qk(starts[0], stage_cumsum(stage_z(starts[0])))
    for p, s in enumerate(starts):
        nxt = starts[p + 1] if p + 1 < len(starts) else None
        q_in, q_part, k_part, k_seen, decay = qk
        if nxt is not None:
            z_n = stage_z(nxt)
        scores, upd = stage_mm(s, q_in, k_part, k_seen)
        if nxt is not None:
            b_n = stage_cumsum(z_n)
        st, o = stage_recur(s, st, q_part, scores, upd, decay)
        if nxt is not None:
            qk = stage_qk(nxt, b_n)
        stage_out(s, o)
    st_ref[...] = st


def _gla(proj, glr, wup_pad, b_alpha2, norm_w2, batch, seq, t_rows=2048, n_parts=8):
    m = proj.shape[0]
    nt = seq // t_rows
    row = lambda b, h, t: b * nt + t
    return pl.pallas_call(
        functools.partial(_gla_kernel, t_rows=t_rows, n_parts=n_parts),
        grid=(batch, GLA_HEADS, nt),
        in_specs=[
            pl.BlockSpec((t_rows, GLA_DK), lambda b, h, t: (row(b, h, t), Q_OFF // GLA_DK + h)),
            pl.BlockSpec((t_rows, GLA_DK), lambda b, h, t: (row(b, h, t), K_OFF // GLA_DK + h)),
            pl.BlockSpec((t_rows, GLA_DV), lambda b, h, t: (row(b, h, t), V_OFF // GLA_DV + h)),
            pl.BlockSpec((t_rows, GLA_DV), lambda b, h, t: (row(b, h, t), GO_OFF // GLA_DV + h)),
            pl.BlockSpec((t_rows, LANE), lambda b, h, t: (row(b, h, t), 0)),
            pl.BlockSpec((LANE, GLA_DK), lambda b, h, t: (0, h)),
            pl.BlockSpec((1, GLA_DK), lambda b, h, t: (0, h)),
            pl.BlockSpec((1, GLA_DV), lambda b, h, t: (0, 0)),
        ],
        out_specs=pl.BlockSpec((t_rows, GLA_DV), lambda b, h, t: (row(b, h, t), h)),
        out_shape=jax.ShapeDtypeStruct((m, GLA_HEADS * GLA_DV), BF16),
        scratch_shapes=[pltpu.VMEM((GLA_DV, GLA_DK), F32)],
        compiler_params=pltpu.CompilerParams(
            dimension_semantics=("arbitrary", "arbitrary", "arbitrary"),
            vmem_limit_bytes=VMEM_LIMIT),
        name="gla",
    )(proj, proj, proj, proj, glr, wup_pad, b_alpha2, norm_w2)


def _alibi_slopes():
    h = np.arange(1, SWA_HEADS + 1, dtype=np.float32)
    return [float(s) for s in (2.0 ** (-8.0 * h / SWA_HEADS)).astype(np.float32)]


MASKED = -1e30
ONES_ROWS = 16
SWA_LOOKAHEAD = 5


def _swa_fill_tables(slope_ref, tbl_ref):
    w = WINDOW
    s = lax.broadcasted_iota(jnp.int32, (2 * w, w), 0)
    t = lax.broadcasted_iota(jnp.int32, (2 * w, w), 1)
    dist = t - s + w
    band = (dist >= 0) & (dist < w)
    ndist = -dist.astype(F32)

    def fill(h, carry):
        bias = (slope_ref[h] * LOG2E) * ndist
        tbl_ref[0, h] = jnp.where(band & (s >= w), bias, MASKED)
        tbl_ref[1, h] = jnp.where(band, bias, MASKED)
        return carry

    lax.fori_loop(0, SWA_HEADS, fill, 0)


def _swa_blocks(n, sink_ref, q_ref, kp_ref, kc_ref, vp_ref, vc_ref, o_ref, tbl_ref,
                q_blocks, between=None):
    w = WINDOW
    lane = lax.broadcasted_iota(jnp.int32, (2 * w, LANE), 1)
    low = lane < SWA_HD
    lane_q = lax.broadcasted_iota(jnp.int32, (w, LANE), 1)
    low_q = lane_q < SWA_HD
    ones = jnp.ones((ONES_ROWS, 2 * w), BF16)

    kall = jnp.concatenate([kp_ref[...], kc_ref[...]], axis=0)
    vall = jnp.concatenate([vp_ref[...], vc_ref[...]], axis=0)
    n_pairs = SWA_HEADS // 2
    pairs_per_kv = SWA_GROUP // 2

    def window(blk):
        kfull = kall[blk * w:(blk + 2) * w, :]
        v_t = vall[blk * w:(blk + 2) * w, :].astype(F32).T.astype(BF16)
        kdups = []
        for kvp in range(SWA_KV_HEADS // 2):
            kcol = kfull[:, kvp * LANE:(kvp + 1) * LANE].astype(F32) * LOG2E
            krol = pltpu.roll(kcol, SWA_HD, 1)
            kdups.append(jnp.where(low, kcol, krol).astype(BF16))
            kdups.append(jnp.where(low, krol, kcol).astype(BF16))
        vaugs = [jnp.concatenate([v_t[kv * SWA_HD:(kv + 1) * SWA_HD, :], ones], axis=0)
                 for kv in range(SWA_KV_HEADS)]
        return kdups, vaugs

    windows = [window(blk) for blk in range(q_blocks)]

    def logits_t(blk, p_idx):
        qp = (q_ref[blk * w:(blk + 1) * w, p_idx * LANE:(p_idx + 1) * LANE]
              * (SWA_HD ** -0.5))
        qw = jnp.concatenate([jnp.where(low_q, qp, 0.0),
                              jnp.where(low_q, 0.0, qp)], axis=0)
        return lax.dot_general(windows[blk][0][p_idx // pairs_per_kv], qw,
                               (((1,), (1,)), ((), ())),
                               preferred_element_type=F32)

    def finish(blk, p_idx, lg):
        var = jnp.minimum(n, 1) if blk == 0 else 1
        ps, ms = [], []
        for e in range(2):
            h = 2 * p_idx + e
            l = lg[:, e * w:(e + 1) * w] + tbl_ref[var, h]
            mx = jnp.maximum(jnp.max(l, axis=0, keepdims=True), sink_ref[h] * LOG2E)
            ps.append(jnp.exp2(l - mx).astype(BF16))
            ms.append(mx)
        ot = jnp.dot(windows[blk][1][p_idx // pairs_per_kv], jnp.concatenate(ps, axis=1),
                     preferred_element_type=F32)
        halves = []
        for e in range(2):
            h = 2 * p_idx + e
            den = (ot[SWA_HD:SWA_HD + 1, e * w:(e + 1) * w]
                   + jnp.exp2(sink_ref[h] * LOG2E - ms[e]))
            halves.append(ot[0:SWA_HD, e * w:(e + 1) * w] * (1.0 / den))
        o_ref[blk * w:(blk + 1) * w, p_idx * LANE:(p_idx + 1) * LANE] = (
            jnp.concatenate(halves, axis=0).T.astype(BF16))

    units = [(blk, p) for blk in range(q_blocks) for p in range(n_pairs)]
    pending = {u: logits_t(*u) for u in units[:SWA_LOOKAHEAD]}
    for idx, u in enumerate(units):
        finish(*u, pending.pop(u))
        if idx + SWA_LOOKAHEAD < len(units):
            nu = units[idx + SWA_LOOKAHEAD]
            pending[nu] = logits_t(*nu)
        if between is not None:
            between(idx)


def _attn_merge_kernel(sink_ref, slope_ref, q_ref, kp_ref, kc_ref, vp_ref, vc_ref,
                       oa_ref, gg_ref, gs_ref, x_ref, wg_ref, ws_ref, wo_ref,
                       g_ref, b_ref, o_ref, tbl_ref, ob_ref, *, q_blocks):
    n = pl.program_id(1)

    @pl.when((pl.program_id(0) == 0) & (n == 0))
    def _():
        _swa_fill_tables(slope_ref, tbl_ref)

    n_units = q_blocks * (SWA_HEADS // 2)
    n_chunks = D_MODEL // MXU_COLS
    every = n_units // n_chunks
    ya_chunks = []

    def between(idx):
        if (idx + 1) % every == 0:
            c0 = len(ya_chunks) * MXU_COLS
            ya_chunks.append(jnp.dot(oa_ref[...], wg_ref[:, c0:c0 + MXU_COLS],
                                     preferred_element_type=F32))

    _swa_blocks(n, sink_ref, q_ref, kp_ref, kc_ref, vp_ref, vc_ref, ob_ref, tbl_ref,
                q_blocks, between)
    ya = jnp.concatenate(ya_chunks, axis=1)
    yb = jnp.dot(ob_ref[...], ws_ref[...], preferred_element_type=F32)
    merged = (jax.nn.sigmoid(gg_ref[...].astype(F32)) * ya
              + jax.nn.sigmoid(gs_ref[...].astype(F32)) * yb)
    mix = jnp.dot(merged.astype(BF16), wo_ref[...], preferred_element_type=F32)
    o_ref[...] = _layer_norm(ALPHA * x_ref[...] + mix, g_ref[...], b_ref[...])


def _attn_merge(proj, sinks, o_a, x2, wg, ws, wo, ln_g, ln_b, batch, seq, q_blocks=2):
    m = proj.shape[0]
    w = WINDOW
    tq = q_blocks * w
    ns = seq // tq
    kvw = SWA_KV_HEADS * SWA_HD
    cur = lambda b, n: b * ns + n
    prev = lambda b, n: (b * ns + n) * q_blocks - jnp.minimum(n, 1)
    rows = lambda col: pl.BlockSpec((tq, D_MODEL), lambda b, n: (cur(b, n), col))
    wspec = pl.BlockSpec((D_MODEL, D_MODEL), lambda b, n: (0, 0),
                         pipeline_mode=pl.Buffered(1))
    vspec = pl.BlockSpec((1, D_MODEL), lambda b, n: (0, 0))
    return pl.pallas_call(
        functools.partial(_attn_merge_kernel, q_blocks=q_blocks),
        grid=(batch, ns),
        in_specs=[
            pl.BlockSpec(memory_space=pltpu.SMEM),
            pl.BlockSpec(memory_space=pltpu.SMEM),
            rows(SQ_OFF // D_MODEL),
            pl.BlockSpec((w, kvw), lambda b, n: (prev(b, n), SK_OFF // kvw)),
            pl.BlockSpec((tq, kvw), lambda b, n: (cur(b, n), SK_OFF // kvw)),
            pl.BlockSpec((w, kvw), lambda b, n: (prev(b, n), SV_OFF // kvw)),
            pl.BlockSpec((tq, kvw), lambda b, n: (cur(b, n), SV_OFF // kvw)),
            rows(0), rows(GG_OFF // D_MODEL), rows(GS_OFF // D_MODEL), rows(0),
            wspec, wspec, wspec, vspec, vspec,
        ],
        out_specs=rows(0),
        out_shape=jax.ShapeDtypeStruct((m, D_MODEL), F32),
        scratch_shapes=[pltpu.VMEM((2, SWA_HEADS, 2 * w, w), F32),
                        pltpu.VMEM((tq, D_MODEL), BF16)],
        compiler_params=pltpu.CompilerParams(
            dimension_semantics=("arbitrary", "arbitrary"),
            vmem_limit_bytes=FFN_VMEM_LIMIT),
        name="attn_merge",
    )(sinks, jnp.asarray(_alibi_slopes(), F32), proj, proj, proj, proj, proj,
      o_a, proj, proj, x2, wg, ws, wo, ln_g, ln_b)


def _ffn_kernel(x_ref, wu_ref, wd_ref, g_ref, b_ref, o_ref, xb_ref):
    j = pl.program_id(1)

    @pl.when(j == 0)
    def _():
        xb_ref[...] = x_ref[...].astype(BF16)
        o_ref[...] = jnp.zeros_like(o_ref)

    h = jnp.dot(xb_ref[...], wu_ref[...], preferred_element_type=F32)
    h = jnp.maximum(h, 0.0)
    h = (h * h).astype(BF16)
    o_ref[...] += jnp.dot(h, wd_ref[...], preferred_element_type=F32)

    @pl.when(j == pl.num_programs(1) - 1)
    def _():
        o_ref[...] = _layer_norm(ALPHA * x_ref[...] + o_ref[...],
                                 g_ref[...], b_ref[...])


def _ffn(x1, wu, wd, ln_g, ln_b, tm=512, tf=2048):
    m = x1.shape[0]
    vspec = pl.BlockSpec((1, D_MODEL), lambda i, j: (0, 0))
    return pl.pallas_call(
        _ffn_kernel,
        grid=(m // tm, D_FF // tf),
        in_specs=[
            pl.BlockSpec((tm, D_MODEL), lambda i, j: (i, 0)),
            pl.BlockSpec((D_MODEL, tf), lambda i, j: (0, j)),
            pl.BlockSpec((tf, D_MODEL), lambda i, j: (j, 0)),
            vspec, vspec,
        ],
        out_specs=pl.BlockSpec((tm, D_MODEL), lambda i, j: (i, 0)),
        out_shape=jax.ShapeDtypeStruct((m, D_MODEL), F32),
        scratch_shapes=[pltpu.VMEM((tm, D_MODEL), BF16)],
        compiler_params=pltpu.CompilerParams(
            dimension_semantics=("arbitrary", "arbitrary"),
            vmem_limit_bytes=FFN_VMEM_LIMIT),
        name="ffn",
    )(x1, wu, wd, ln_g, ln_b)


REPACK_COLS = 512
GLA_COLS = 6144


def _repack_src_block(t):
    first_gate = GG_OFF // REPACK_COLS
    first_kv = SK_OFF // REPACK_COLS
    return jnp.where(t < first_gate, t,
                     jnp.where(t < first_kv, t + 1, SQ_OFF // REPACK_COLS + D_MODEL // REPACK_COLS))


def _repack_kernel(a_ref, b_ref, o_ref):
    t = pl.program_id(0)
    n_aligned = GLA_COLS // REPACK_COLS

    @pl.when(t < n_aligned)
    def _():
        o_ref[...] = a_ref[...].astype(BF16)

    @pl.when(t >= n_aligned)
    def _():
        ar = pltpu.roll(a_ref[...], REPACK_COLS - GLA_RANK, 1)
        br = pltpu.roll(b_ref[...], LANE - GLA_RANK, 1)
        lane = lax.broadcasted_iota(jnp.int32, br.shape, 1)
        last = jnp.where(lane < LANE - GLA_RANK, ar[:, REPACK_COLS - LANE:], br)
        o_ref[...] = jnp.concatenate([ar[:, :REPACK_COLS - LANE], last], axis=1).astype(BF16)


def _repack_w_in(w_in):
    first = GLA_COLS // REPACK_COLS + 1
    n_strips = (w_in.shape[1] - GLA_RANK) // REPACK_COLS - first + 1
    strips = jnp.concatenate(
        [jnp.pad(w_in[:, REPACK_COLS * k:REPACK_COLS * k + GLA_RANK],
                 ((0, 0), (0, LANE - GLA_RANK)))
         for k in range(first, first + n_strips)], axis=1)
    return pl.pallas_call(
        _repack_kernel,
        grid=(N_PROJ // REPACK_COLS,),
        in_specs=[
            pl.BlockSpec((D_MODEL, REPACK_COLS), lambda t: (0, _repack_src_block(t))),
            pl.BlockSpec((D_MODEL, LANE),
                         lambda t: (0, jnp.clip(_repack_src_block(t) + 1 - first,
                                                0, n_strips - 1))),
        ],
        out_specs=pl.BlockSpec((D_MODEL, REPACK_COLS), lambda t: (0, t)),
        out_shape=jax.ShapeDtypeStruct((D_MODEL, N_PROJ), BF16),
        compiler_params=pltpu.CompilerParams(
            dimension_semantics=("arbitrary",),
            vmem_limit_bytes=VMEM_LIMIT),
        name="repack",
    )(w_in, strips)


def kernel(x, w_in, w_alpha_up, b_alpha, gla_norm_w, attn_sinks, w_branch_gla,
           w_branch_swa, w_out, ln1_g, ln1_b, w_ff_up, w_ff_down, ln2_g, ln2_b):
    batch, seq, d = x.shape
    x2 = x.reshape(batch * seq, d)

    w_main = _repack_w_in(w_in)
    w_lr = jnp.pad(w_in[:, 6144:6160], ((0, 0), (0, LANE - GLA_RANK))).astype(BF16)
    wup_pad = jnp.pad(w_alpha_up, ((0, LANE - GLA_RANK), (0, 0))).astype(BF16)

    proj, glr = _proj(x2, w_main, w_lr)
    o_a = _gla(proj, glr, wup_pad, b_alpha.reshape(1, -1).astype(F32),
               gla_norm_w.reshape(1, -1).astype(F32), batch, seq)
    x1 = _attn_merge(proj, attn_sinks.astype(F32), o_a, x2, w_branch_gla.astype(BF16),
                     w_branch_swa.astype(BF16), w_out.astype(BF16),
                     ln1_g.reshape(1, -1), ln1_b.reshape(1, -1), batch, seq)
    out = _ffn(x1, w_ff_up.astype(BF16), w_ff_down.astype(BF16),
               ln2_g.reshape(1, -1), ln2_b.reshape(1, -1))
    return out.reshape(batch, seq, d)
```
